```python
import math
import jax, jax.numpy as jnp
from jax import lax
import numpy as np

D_MODEL = 2048
BATCH = 1
SEQ = 8192
DEPTH = 1
DEC_BATCH = 8
DEC_SEQ = 4096
PAST_LEN = 128

N_HEADS = 16
N_KV_HEADS = 4
HEAD_DIM = 64
Q_DIM = N_HEADS * HEAD_DIM
KV_DIM = N_KV_HEADS * HEAD_DIM
WINDOW = 128
BLOCK = 128
ROPE_THETA = 10000.0
CONV_DIM = 1024
CONV_K = 31
D_FF = int(math.ceil(8 * D_MODEL / 3 / 256) * 256)
N_MOD = 6
EPS = 1e-6
IN_SPLITS = (Q_DIM, Q_DIM + KV_DIM, Q_DIM + 2 * KV_DIM,
             Q_DIM + 2 * KV_DIM + 2 * CONV_DIM,
             Q_DIM + 2 * KV_DIM + 2 * CONV_DIM + D_MODEL)
IN_DIM = Q_DIM + 2 * KV_DIM + 2 * CONV_DIM + 2 * D_MODEL

kernel_name = "hybrid_swa_conformer_adaln_encoder"


def rmsnorm(x, g):
    xf = x.astype(jnp.float32)
    y = xf * lax.rsqrt(jnp.mean(xf * xf, axis=-1, keepdims=True) + EPS)
    return (y * g.astype(jnp.float32)).astype(x.dtype)


def layernorm(x, g, b):
    xf = x.astype(jnp.float32)
    mu = jnp.mean(xf, axis=-1, keepdims=True)
    var = jnp.mean(jnp.square(xf - mu), axis=-1, keepdims=True)
    y = (xf - mu) * lax.rsqrt(var + EPS)
    return (y * g.astype(jnp.float32) + b.astype(jnp.float32)).astype(x.dtype)


def rope(x):
    S = x.shape[1]
    inv = 1.0 / (ROPE_THETA ** (jnp.arange(0, HEAD_DIM, 2, dtype=jnp.float32) / HEAD_DIM))
    ang = jnp.arange(S, dtype=jnp.float32)[:, None] * inv[None, :]
    cos = jnp.cos(ang)[None, :, None, :]
    sin = jnp.sin(ang)[None, :, None, :]
    xf = x.astype(jnp.float32)
    x1, x2 = xf[..., : HEAD_DIM // 2], xf[..., HEAD_DIM // 2:]
    return jnp.concatenate([x1 * cos - x2 * sin, x2 * cos + x1 * sin], axis=-1).astype(x.dtype)


def band_blocks(t, nb):
    B = t.shape[0]
    tp = jnp.pad(t, ((0, 0), (BLOCK, BLOCK), (0, 0), (0, 0)))
    tp = tp.reshape(B, nb + 2, BLOCK, N_KV_HEADS, HEAD_DIM)
    return jnp.concatenate([tp[:, :-2], tp[:, 1:-1], tp[:, 2:]], axis=2)


def window_attention(q, k, v, sink):
    B, S = q.shape[0], q.shape[1]
    nb = S // BLOCK
    G = N_HEADS // N_KV_HEADS
    qb = q.reshape(B, nb, BLOCK, N_KV_HEADS, G, HEAD_DIM)
    kb = band_blocks(k, nb)
    vb = band_blocks(v, nb)
    s = jnp.einsum('bnqkgd,bnjkd->bnkgqj', qb, kb).astype(jnp.float32) * (HEAD_DIM ** -0.5)
    qi = jnp.arange(BLOCK)[:, None]
    kj = jnp.arange(3 * BLOCK)[None, :]
    rel = kj - BLOCK - qi
    kpos = jnp.arange(nb)[:, None, None] * BLOCK - BLOCK + kj[None]
    valid = (jnp.abs(rel)[None] <= WINDOW) & (kpos >= 0) & (kpos < S)
    s = jnp.where(valid[None, :, None, None], s, -jnp.inf)
    sk = sink.astype(jnp.float32).reshape(1, 1, N_KV_HEADS, G, 1, 1)
    m = jnp.maximum(jnp.max(s, axis=-1, keepdims=True), sk)
    p = jnp.exp(s - m)
    denom = jnp.sum(p, axis=-1, keepdims=True) + jnp.exp(sk - m)
    p = (p / denom).astype(v.dtype)
    o = jnp.einsum('bnkgqj,bnjkd->bnqkgd', p, vb)
    return o.reshape(B, S, N_HEADS * HEAD_DIM)


def conformer_conv(u, conv_w, conv_b, ln_g, ln_b):
    a, g = jnp.split(u, 2, axis=-1)
    h = a * jax.nn.sigmoid(g)
    h = lax.conv_general_dilated(h, conv_w[:, None, :].astype(h.dtype), window_strides=(1,),
                                 padding=[(CONV_K // 2, CONV_K // 2)],
                                 dimension_numbers=('NWC', 'WIO', 'NWC'),
                                 feature_group_count=CONV_DIM)
    h = h + conv_b
    h = layernorm(h, ln_g, ln_b)
    return jax.nn.silu(h)


def encoder(x, c, w_ada, b_ada, g_mix, w_in, attn_sink, w_attn_o, conv_w, conv_b,
            conv_ln_g, conv_ln_b, w_conv_o, w_out, g_ffn, w_ffn_in, w_ffn_out, g_final):
    B, S, _ = x.shape
    for l in range(DEPTH):
        mod = jax.nn.silu(c) @ w_ada[l] + b_ada[l]
        sh1, sc1, gt1, sh2, sc2, gt2 = [t[:, None, :] for t in jnp.split(mod, N_MOD, axis=-1)]
        h = rmsnorm(x, g_mix[l]) * (1.0 + sc1) + sh1
        proj = h @ w_in[l]
        q, k, v, u, ga, gc = jnp.split(proj, IN_SPLITS, axis=-1)
        q = rope(q.reshape(B, S, N_HEADS, HEAD_DIM))
        k = rope(k.reshape(B, S, N_KV_HEADS, HEAD_DIM))
        v = v.reshape(B, S, N_KV_HEADS, HEAD_DIM)
        attn = window_attention(q, k, v, attn_sink[l]) @ w_attn_o[l]
        conv = conformer_conv(u, conv_w[l], conv_b[l], conv_ln_g[l], conv_ln_b[l]) @ w_conv_o[l]
        mix = jax.nn.sigmoid(ga) * attn + jax.nn.sigmoid(gc) * conv
        x = x + gt1 * (mix @ w_out[l])
        h = rmsnorm(x, g_ffn[l]) * (1.0 + sc2) + sh2
        a, b = jnp.split(h @ w_ffn_in[l], 2, axis=-1)
        x = x + gt2 * ((jax.nn.silu(a) * b) @ w_ffn_out[l])
    return rmsnorm(x, g_final)


def setup_inputs(seed: int = 0) -> dict:
    key = jax.random.key(seed)
    ks = jax.random.split(key, 24)
    f32 = jnp.float32

    def nrm(k, shape, scale):
        return jax.random.normal(k, shape, f32) * scale

    L = DEPTH
    return {
        "x_prompt": nrm(ks[0], (BATCH, SEQ, D_MODEL), 1.0),
        "x_sample": nrm(ks[1], (DEC_BATCH, DEC_SEQ, D_MODEL), 1.0),
        "c_prompt": nrm(ks[2], (BATCH, D_MODEL), 1.0),
        "c_sample": nrm(ks[3], (DEC_BATCH, D_MODEL), 1.0),
        "w_ada": nrm(ks[4], (L, D_MODEL, N_MOD * D_MODEL), 0.5 * D_MODEL ** -0.5),
        "b_ada": nrm(ks[5], (L, N_MOD * D_MODEL), 0.01),
        "g_mix": 1.0 + nrm(ks[6], (L, D_MODEL), 0.01),
        "w_in": nrm(ks[7], (L, D_MODEL, IN_DIM), D_MODEL ** -0.5),
        "attn_sink": nrm(ks[8], (L, N_HEADS), 0.5),
        "w_attn_o": nrm(ks[9], (L, Q_DIM, D_MODEL), Q_DIM ** -0.5),
        "conv_w": nrm(ks[10], (L, CONV_K, CONV_DIM), CONV_K ** -0.5),
        "conv_b": nrm(ks[11], (L, CONV_DIM), 0.01),
        "conv_ln_g": 1.0 + nrm(ks[12], (L, CONV_DIM), 0.01),
        "conv_ln_b": nrm(ks[13], (L, CONV_DIM), 0.01),
        "w_conv_o": nrm(ks[14], (L, CONV_DIM, D_MODEL), CONV_DIM ** -0.5),
        "w_out": nrm(ks[15], (L, D_MODEL, D_MODEL), D_MODEL ** -0.5),
        "g_ffn": 1.0 + nrm(ks[16], (L, D_MODEL), 0.01),
        "w_ffn_in": nrm(ks[17], (L, D_MODEL, 2 * D_FF), D_MODEL ** -0.5),
        "w_ffn_out": nrm(ks[18], (L, D_FF, D_MODEL), D_FF ** -0.5),
        "g_final": 1.0 + nrm(ks[19], (D_MODEL,), 0.01),
    }


def reference(x_prompt, x_sample, c_prompt, c_sample, w_ada, b_ada, g_mix, w_in, attn_sink,
              w_attn_o, conv_w, conv_b, conv_ln_g, conv_ln_b, w_conv_o, w_out, g_ffn,
              w_ffn_in, w_ffn_out, g_final):
    y_prompt = encoder(x_prompt, c_prompt, w_ada, b_ada, g_mix, w_in, attn_sink, w_attn_o,
                       conv_w, conv_b, conv_ln_g, conv_ln_b, w_conv_o, w_out, g_ffn,
                       w_ffn_in, w_ffn_out, g_final)
    y_sample = encoder(x_sample, c_sample, w_ada, b_ada, g_mix, w_in, attn_sink, w_attn_o,
                       conv_w, conv_b, conv_ln_g, conv_ln_b, w_conv_o, w_out, g_ffn,
                       w_ffn_in, w_ffn_out, g_final)
    return (y_prompt, y_sample)
```

```python
import functools
import math

import jax
import jax.numpy as jnp
from jax import lax
from jax.experimental import pallas as pl
from jax.experimental.pallas import tpu as pltpu

D_MODEL = 2048
N_HEADS = 16
N_KV_HEADS = 4
GROUP = N_HEADS // N_KV_HEADS
HEAD_DIM = 64
Q_DIM = N_HEADS * HEAD_DIM
KV_DIM = N_KV_HEADS * HEAD_DIM
BLOCK = 128
ROPE_THETA = 10000.0
CONV_DIM = 1024
CONV_K = 31
CONV_PAD = CONV_K // 2
D_FF = int(math.ceil(8 * D_MODEL / 3 / 256) * 256)
N_MOD = 6
EPS = 1e-6
IN_DIM = Q_DIM + 2 * KV_DIM + 2 * CONV_DIM + 2 * D_MODEL

LANES = 128
HALO = 16
QKV_TILE = Q_DIM + 2 * KV_DIM
KV_REP = GROUP * HEAD_DIM
QKV_OUT = Q_DIM + 2 * N_KV_HEADS * KV_REP

VMEM_LIMIT = 56 * 1024 * 1024

F32 = jnp.float32
BF16 = jnp.bfloat16


def _sigmoid(x):
    return 1.0 / (1.0 + jnp.exp(-x))


def _params(sem):
    return pltpu.CompilerParams(dimension_semantics=sem, vmem_limit_bytes=VMEM_LIMIT)


def _mod_kernel(c_ref, w_ref, b_ref, o_ref):
    c = c_ref[...]
    sc = c * _sigmoid(c)
    o_ref[...] = jnp.dot(sc, w_ref[...], precision=lax.Precision.HIGHEST,
                         preferred_element_type=F32) + b_ref[...]


def _mod_call(c_pad, w_ada, b_ada):
    rows = c_pad.shape[0]
    n = w_ada.shape[1]
    tn = 1024
    return pl.pallas_call(
        _mod_kernel,
        grid=(n // tn,),
        in_specs=[pl.BlockSpec((rows, D_MODEL), lambda j: (0, 0)),
                  pl.BlockSpec((D_MODEL, tn), lambda j: (0, j)),
                  pl.BlockSpec((1, tn), lambda j: (0, j))],
        out_specs=pl.BlockSpec((rows, tn), lambda j: (0, j)),
        out_shape=jax.ShapeDtypeStruct((rows, n), F32),
        compiler_params=_params(("arbitrary",)),
        name="adaln_mod",
    )(c_pad, w_ada, b_ada.reshape(1, n))


def _inproj_kernel(x_ref, sc_ref, sh_ref, g_ref, cos_ref, sin_ref, w_ref, ugg_ref, qkv_ref, h_ref):
    j = pl.program_id(2)

    @pl.when(j == 0)
    def _():
        x = x_ref[...]
        ms = jnp.sum(x * x, axis=-1, keepdims=True) * (1.0 / D_MODEL)
        y = x * lax.rsqrt(ms + EPS) * g_ref[...]
        h_ref[...] = (y * (1.0 + sc_ref[...]) + sh_ref[...]).astype(BF16)

    acc = jnp.dot(h_ref[...], w_ref[...], preferred_element_type=F32)

    @pl.when(j > 0)
    def _():
        ugg_ref[...] = acc.astype(BF16)

    @pl.when(j == 0)
    def _():
        lane = lax.broadcasted_iota(jnp.int32, (1, LANES), 1)
        first_half = (lane % HEAD_DIM) < (HEAD_DIM // 2)
        low_head = lane < HEAD_DIM
        cos = cos_ref[...]
        sin = sin_ref[...]

        def rope(xc):
            partner = jnp.where(first_half, pltpu.roll(xc, LANES - HEAD_DIM // 2, 1),
                                pltpu.roll(xc, HEAD_DIM // 2, 1))
            return xc * cos + partner * sin

        def store_replicated(yc, pair, base):
            sw = pltpu.roll(yc, HEAD_DIM, 1)
            for half, rep in ((0, jnp.where(low_head, yc, sw)), (1, jnp.where(low_head, sw, yc))):
                rep = rep.astype(BF16)
                col = base + (2 * pair + half) * KV_REP
                for r in range(KV_REP // LANES):
                    qkv_ref[:, col + r * LANES:col + (r + 1) * LANES] = rep

        for c in range(Q_DIM // LANES):
            yc = rope(acc[:, c * LANES:(c + 1) * LANES]) * (HEAD_DIM ** -0.5)
            qkv_ref[:, c * LANES:(c + 1) * LANES] = yc.astype(BF16)
        for c in range(KV_DIM // LANES):
            kc = rope(acc[:, Q_DIM + c * LANES:Q_DIM + (c + 1) * LANES])
            store_replicated(kc, c, Q_DIM)
            vc = acc[:, Q_DIM + KV_DIM + c * LANES:Q_DIM + KV_DIM + (c + 1) * LANES]
            store_replicated(vc, c, Q_DIM + N_KV_HEADS * KV_REP)


def _inproj_call(x, mod, g_mix, cos_t, sin_t, w_in, tm):
    B, S, _ = x.shape
    tn = QKV_TILE
    n_tiles = IN_DIM // tn
    vec = lambda k: pl.BlockSpec((None, None, 1, D_MODEL), lambda b, i, j: (b, k, 0, 0))
    return pl.pallas_call(
        _inproj_kernel,
        grid=(B, S // tm, n_tiles),
        in_specs=[pl.BlockSpec((None, tm, D_MODEL), lambda b, i, j: (b, i, 0)),
                  vec(1), vec(0),
                  pl.BlockSpec((1, D_MODEL), lambda b, i, j: (0, 0)),
                  pl.BlockSpec((tm, LANES), lambda b, i, j: (i, 0)),
                  pl.BlockSpec((tm, LANES), lambda b, i, j: (i, 0)),
                  pl.BlockSpec((D_MODEL, tn), lambda b, i, j: (0, j))],
        out_specs=[pl.BlockSpec((None, tm, tn), lambda b, i, j: (b, i, jnp.maximum(j - 1, 0))),
                   pl.BlockSpec((None, tm, QKV_OUT), lambda b, i, j: (b, i, 0))],
        out_shape=[jax.ShapeDtypeStruct((B, S, IN_DIM - tn), BF16),
                   jax.ShapeDtypeStruct((B, S, QKV_OUT), BF16)],
        scratch_shapes=[pltpu.VMEM((tm, D_MODEL), BF16)],
        compiler_params=_params(("arbitrary", "arbitrary", "arbitrary")),
        name="inproj",
    )(x, mod, mod, g_mix, cos_t, sin_t, w_in)


def _attn_kernel(sink_ref, q_ref, kp_ref, km_ref, kn_ref, vp_ref, vm_ref, vn_ref, o_ref,
                 kbuf, vbuf, *, tq):
    i = pl.program_id(1)
    nq = tq // BLOCK
    last_blk = pl.num_programs(1) * nq - 1
    rows = GROUP * BLOCK

    kbuf[0:BLOCK] = kp_ref[...]
    kbuf[BLOCK:BLOCK + tq] = km_ref[...]
    kbuf[BLOCK + tq:] = kn_ref[...]
    vbuf[0:BLOCK] = vp_ref[...]
    vbuf[BLOCK:BLOCK + tq] = vm_ref[...]
    vbuf[BLOCK + tq:] = vn_ref[...]

    qi = lax.broadcasted_iota(jnp.int32, (rows, BLOCK), 0) % BLOCK
    kj = lax.broadcasted_iota(jnp.int32, (rows, BLOCK), 1)
    neg = jnp.float32(-jnp.inf)
    bias_prev = jnp.where(kj >= qi, 0.0, neg)
    bias_next = jnp.where(kj <= qi, 0.0, neg)
    lane_grp = lax.broadcasted_iota(jnp.int32, (BLOCK, KV_REP), 1) // HEAD_DIM
    grp_mask = [lane_grp == g for g in range(GROUP)]
    nt = (((1,), (1,)), ((), ()))

    for qb in range(nq):
        blk = i * nq + qb
        bp = jnp.where(blk > 0, bias_prev, neg)
        bn = jnp.where(blk < last_blk, bias_next, neg)
        r0 = qb * BLOCK
        for kh in range(N_KV_HEADS):
            c0 = kh * KV_REP
            q_kh = q_ref[r0:r0 + BLOCK, c0:c0 + KV_REP]
            zero = jnp.zeros_like(q_kh)
            q4 = jnp.concatenate([jnp.where(grp_mask[g], q_kh, zero) for g in range(GROUP)], axis=0)
            ks = [kbuf[r0 + p * BLOCK:r0 + (p + 1) * BLOCK, c0:c0 + KV_REP] for p in range(3)]
            vs = [vbuf[r0 + p * BLOCK:r0 + (p + 1) * BLOCK, c0:c0 + KV_REP] for p in range(3)]
            s0 = lax.dot_general(q4, ks[0], nt, preferred_element_type=F32) + bp
            s1 = lax.dot_general(q4, ks[1], nt, preferred_element_type=F32)
            s2 = lax.dot_general(q4, ks[2], nt, preferred_element_type=F32) + bn
            sink = jnp.concatenate(
                [jnp.full((BLOCK, 1), sink_ref[kh * GROUP + g], F32) for g in range(GROUP)], axis=0)
            m = jnp.maximum(jnp.max(jnp.maximum(jnp.maximum(s0, s1), s2), axis=-1, keepdims=True), sink)
            p0 = jnp.exp(s0 - m)
            p1 = jnp.exp(s1 - m)
            p2 = jnp.exp(s2 - m)
            den = jnp.sum(p0 + p1 + p2, axis=-1, keepdims=True) + jnp.exp(sink - m)
            o = (jnp.dot(p0.astype(BF16), vs[0], preferred_element_type=F32)
                 + jnp.dot(p1.astype(BF16), vs[1], preferred_element_type=F32)
                 + jnp.dot(p2.astype(BF16), vs[2], preferred_element_type=F32))
            o = o * (1.0 / den)
            out = jnp.where(grp_mask[0], o[0:BLOCK], 0.0)
            for g in range(1, GROUP):
                out = out + jnp.where(grp_mask[g], o[g * BLOCK:(g + 1) * BLOCK], 0.0)
            o_ref[r0:r0 + BLOCK, c0:c0 + KV_REP] = out.astype(BF16)


def _attn_call(qkv, sink, tq):
    B, S, _ = qkv.shape
    nq = tq // BLOCK
    nb = S // BLOCK
    main = lambda col: pl.BlockSpec((None, tq, Q_DIM), lambda b, i: (b, i, col))
    prev = lambda col: pl.BlockSpec((None, BLOCK, Q_DIM),
                                    lambda b, i: (b, jnp.maximum(i * nq - 1, 0), col))
    nxt = lambda col: pl.BlockSpec((None, BLOCK, Q_DIM),
                                   lambda b, i: (b, jnp.minimum((i + 1) * nq, nb - 1), col))
    return pl.pallas_call(
        functools.partial(_attn_kernel, tq=tq),
        grid=(B, S // tq),
        in_specs=[pl.BlockSpec(memory_space=pltpu.SMEM),
                  main(0), prev(1), main(1), nxt(1), prev(2), main(2), nxt(2)],
        out_specs=pl.BlockSpec((None, tq, Q_DIM), lambda b, i: (b, i, 0)),
        out_shape=jax.ShapeDtypeStruct((B, S, Q_DIM), BF16),
        scratch_shapes=[pltpu.VMEM((tq + 2 * BLOCK, Q_DIM), BF16),
                        pltpu.VMEM((tq + 2 * BLOCK, Q_DIM), BF16)],
        compiler_params=_params(("arbitrary", "arbitrary")),
        name="band_attn",
    )(sink, qkv, qkv, qkv, qkv, qkv, qkv, qkv)


def _mixer_kernel(x_ref, a_ref, g_ref, ap_ref, gp_ref, an_ref, gn_ref, ga_ref, gc_ref, attn_ref,
                  gt1_ref, sc2_ref, sh2_ref, cw_ref, cb_ref, lg_ref, lb_ref, gf_ref,
                  wao_ref, wco_ref, wout_ref, x1_ref, h2_ref, hbuf, cbuf, *, tm):
    i = pl.program_id(1)
    last = pl.num_programs(1) - 1

    def glu(a, g):
        return a.astype(F32) * _sigmoid(g.astype(F32))

    zero_halo = jnp.zeros((HALO, CONV_DIM), F32)
    hbuf[0:HALO] = jnp.where(i > 0, glu(ap_ref[...], gp_ref[...]), zero_halo)
    hbuf[HALO:HALO + tm] = glu(a_ref[...], g_ref[...])
    hbuf[HALO + tm:] = jnp.where(i < last, glu(an_ref[...], gn_ref[...]), zero_halo)

    rc = 64
    for r0 in range(0, tm, rc):
        for c in range(CONV_DIM // LANES):
            cs = slice(c * LANES, (c + 1) * LANES)
            acc = jnp.zeros((rc, LANES), F32)
            for k in range(CONV_K):
                off = r0 + HALO - CONV_PAD + k
                acc = acc + hbuf[off:off + rc, cs] * cw_ref[k:k + 1, cs]
            cbuf[r0:r0 + rc, cs] = acc

    cv = cbuf[...] + cb_ref[...]
    mu = jnp.sum(cv, axis=-1, keepdims=True) * (1.0 / CONV_DIM)
    d = cv - mu
    var = jnp.sum(d * d, axis=-1, keepdims=True) * (1.0 / CONV_DIM)
    ln = d * lax.rsqrt(var + EPS) * lg_ref[...] + lb_ref[...]
    act = (ln * _sigmoid(ln)).astype(BF16)

    conv_o = jnp.dot(act, wco_ref[...], preferred_element_type=F32)
    attn_o = jnp.dot(attn_ref[...], wao_ref[...], preferred_element_type=F32)
    mix = (_sigmoid(ga_ref[...].astype(F32)) * attn_o
           + _sigmoid(gc_ref[...].astype(F32)) * conv_o).astype(BF16)
    y = jnp.dot(mix, wout_ref[...], preferred_element_type=F32)
    x1 = x_ref[...] + gt1_ref[...] * y
    x1_ref[...] = x1
    ms = jnp.sum(x1 * x1, axis=-1, keepdims=True) * (1.0 / D_MODEL)
    n = x1 * lax.rsqrt(ms + EPS) * gf_ref[...]
    h2_ref[...] = (n * (1.0 + sc2_ref[...]) + sh2_ref[...]).astype(BF16)


def _mixer_call(x, ugg, attn, mod, conv_w, conv_b, ln_g, ln_b, g_ffn, wao, wco, wout, tm):
    B, S, _ = x.shape
    hb = tm // HALO
    nh = S // HALO
    row = lambda w, col: pl.BlockSpec((None, tm, w), lambda b, i: (b, i, col))
    prev = lambda col: pl.BlockSpec((None, HALO, CONV_DIM),
                                    lambda b, i: (b, jnp.maximum(i * hb - 1, 0), col))
    nxt = lambda col: pl.BlockSpec((None, HALO, CONV_DIM),
                                   lambda b, i: (b, jnp.minimum((i + 1) * hb, nh - 1), col))
    vec = lambda k: pl.BlockSpec((None, None, 1, D_MODEL), lambda b, i: (b, k, 0, 0))
    const = lambda shape: pl.BlockSpec(shape, lambda b, i: (0, 0), pipeline_mode=pl.Buffered(1))
    return pl.pallas_call(
        functools.partial(_mixer_kernel, tm=tm),
        grid=(B, S // tm),
        in_specs=[row(D_MODEL, 0), row(CONV_DIM, 0), row(CONV_DIM, 1),
                  prev(0), prev(1), nxt(0), nxt(1),
                  row(D_MODEL, 1), row(D_MODEL, 2), row(Q_DIM, 0),
                  vec(2), vec(4), vec(3),
                  const((CONV_K + 1, CONV_DIM)), const((1, CONV_DIM)), const((1, CONV_DIM)),
                  const((1, CONV_DIM)), const((1, D_MODEL)),
                  const((Q_DIM, D_MODEL)), const((CONV_DIM, D_MODEL)), const((D_MODEL, D_MODEL))],
        out_specs=[row(D_MODEL, 0), row(D_MODEL, 0)],
        out_shape=[jax.ShapeDtypeStruct((B, S, D_MODEL), F32),
                   jax.ShapeDtypeStruct((B, S, D_MODEL), BF16)],
        scratch_shapes=[pltpu.VMEM((tm + 2 * HALO, CONV_DIM), F32),
                        pltpu.VMEM((tm, CONV_DIM), F32)],
        compiler_params=_params(("arbitrary", "arbitrary")),
        name="mixer_tail",
    )(x, ugg, ugg, ugg, ugg, ugg, ugg, ugg, ugg, attn, mod, mod, mod,
      conv_w, conv_b, ln_g, ln_b, g_ffn, wao, wco, wout)


def _ffn_kernel(h_ref, x1_ref, gt2_ref, gfin_ref, wa_ref, wb_ref, wo_ref, o_ref, *, final_norm):
    f = pl.program_id(2)
    h = h_ref[...]
    a = jnp.dot(h, wa_ref[...], preferred_element_type=F32)
    b = jnp.dot(h, wb_ref[...], preferred_element_type=F32)
    g = (a * _sigmoid(a) * b).astype(BF16)
    y = gt2_ref[...] * jnp.dot(g, wo_ref[...], preferred_element_type=F32)

    @pl.when(f == 0)
    def _():
        o_ref[...] = x1_ref[...] + y

    @pl.when(f > 0)
    def _():
        o_ref[...] += y

    if final_norm:
        @pl.when(f == pl.num_programs(2) - 1)
        def _():
            x = o_ref[...]
            ms = jnp.sum(x * x, axis=-1, keepdims=True) * (1.0 / D_MODEL)
            o_ref[...] = x * lax.rsqrt(ms + EPS) * gfin_ref[...]


def _ffn_call(h2, x1, mod, g_final, w_ffn_in, w_ffn_out, tm, tf, final_norm):
    B, S, _ = x1.shape
    nf = D_FF // tf
    row = lambda: pl.BlockSpec((None, tm, D_MODEL), lambda b, i, f: (b, i, 0))
    return pl.pallas_call(
        functools.partial(_ffn_kernel, final_norm=final_norm),
        grid=(B, S // tm, nf),
        in_specs=[row(), row(),
                  pl.BlockSpec((None, None, 1, D_MODEL), lambda b, i, f: (b, 5, 0, 0)),
                  pl.BlockSpec((1, D_MODEL), lambda b, i, f: (0, 0)),
                  pl.BlockSpec((D_MODEL, tf), lambda b, i, f: (0, f)),
                  pl.BlockSpec((D_MODEL, tf), lambda b, i, f: (0, f + nf)),
                  pl.BlockSpec((tf, D_MODEL), lambda b, i, f: (f, 0))],
        out_specs=row(),
        out_shape=jax.ShapeDtypeStruct((B, S, D_MODEL), F32),
        compiler_params=_params(("arbitrary", "arbitrary", "arbitrary")),
        name="swiglu_ffn",
    )(h2, x1, mod, g_final, w_ffn_in, w_ffn_in, w_ffn_out)


def _rope_tables(seq):
    inv = 1.0 / (ROPE_THETA ** (jnp.arange(0, HEAD_DIM, 2, dtype=F32) / HEAD_DIM))
    ang = jnp.arange(seq, dtype=F32)[:, None] * inv[None, :]
    cos, sin = jnp.cos(ang), jnp.sin(ang)
    reps = LANES // HEAD_DIM
    cos_t = jnp.tile(jnp.concatenate([cos, cos], axis=-1), (1, reps))
    sin_t = jnp.tile(jnp.concatenate([-sin, sin], axis=-1), (1, reps))
    return cos_t, sin_t


def _encoder(x, mod, layers, g_final):
    B, S, _ = x.shape
    cos_t, sin_t = _rope_tables(S)
    for l, lw in enumerate(layers):
        m = mod[l]
        ugg, qkv = _inproj_call(x, m, lw["g_mix"], cos_t, sin_t, lw["w_in"], tm=512)
        attn = _attn_call(qkv, lw["sink"], tq=512)
        x1, h2 = _mixer_call(x, ugg, attn, m, lw["conv_w"], lw["conv_b"], lw["ln_g"], lw["ln_b"],
                             lw["g_ffn"], lw["wao"], lw["wco"], lw["wout"], tm=256)
        x = _ffn_call(h2, x1, m, g_final, lw["w_ffn_in"], lw["w_ffn_out"], tm=512, tf=512,
                      final_norm=(l == len(layers) - 1))
    return x


def kernel(x_prompt, x_sample, c_prompt, c_sample, w_ada, b_ada, g_mix, w_in, attn_sink, w_attn_o,
           conv_w, conv_b, conv_ln_g, conv_ln_b, w_conv_o, w_out, g_ffn, w_ffn_in, w_ffn_out, g_final):
    depth = w_in.shape[0]
    nb_p, nb_s = c_prompt.shape[0], c_sample.shape[0]
    rows = -(-(nb_p + nb_s) // 8) * 8
    c_all = jnp.concatenate([c_prompt, c_sample], axis=0)
    c_pad = jnp.pad(c_all, ((0, rows - nb_p - nb_s), (0, 0)))

    layers, mods_p, mods_s = [], [], []
    for l in range(depth):
        mod = _mod_call(c_pad, w_ada[l], b_ada[l]).reshape(rows, N_MOD, 1, D_MODEL)
        mods_p.append(mod[:nb_p])
        mods_s.append(mod[nb_p:nb_p + nb_s])
        layers.append(dict(
            g_mix=g_mix[l].reshape(1, D_MODEL), w_in=w_in[l].astype(BF16), sink=attn_sink[l],
            wao=w_attn_o[l].astype(BF16),
            conv_w=jnp.pad(conv_w[l], ((0, 1), (0, 0))), conv_b=conv_b[l].reshape(1, CONV_DIM),
            ln_g=conv_ln_g[l].reshape(1, CONV_DIM), ln_b=conv_ln_b[l].reshape(1, CONV_DIM),
            wco=w_conv_o[l].astype(BF16), wout=w_out[l].astype(BF16),
            g_ffn=g_ffn[l].reshape(1, D_MODEL), w_ffn_in=w_ffn_in[l].astype(BF16),
            w_ffn_out=w_ffn_out[l].astype(BF16)))
    gfin = g_final.reshape(1, D_MODEL)
    y_prompt = _encoder(x_prompt, mods_p, layers, gfin)
    y_sample = _encoder(x_sample, mods_s, layers, gfin)
    return (y_prompt, y_sample)
```

```python
import functools
import math

import jax
import jax.numpy as jnp
from jax import lax
from jax.experimental import pallas as pl
from jax.experimental.pallas import tpu as pltpu

D_MODEL = 2048
N_HEADS = 16
N_KV_HEADS = 4
GROUP = N_HEADS // N_KV_HEADS
HEAD_DIM = 64
Q_DIM = N_HEADS * HEAD_DIM
KV_DIM = N_KV_HEADS * HEAD_DIM
BLOCK = 128
WINDOW = 128
ROPE_THETA = 10000.0
CONV_DIM = 1024
CONV_K = 31
CONV_PAD = CONV_K // 2
D_FF = int(math.ceil(8 * D_MODEL / 3 / 256) * 256)
N_MOD = 6
EPS = 1e-6
IN_DIM = Q_DIM + 2 * KV_DIM + 2 * CONV_DIM + 2 * D_MODEL

LANES = 128
SUBLANES = 8
MXU_N = 256
HALO = 16
QKV_TILE = Q_DIM + 2 * KV_DIM
Q_ROWS = GROUP * HEAD_DIM

VMEM_LIMIT = 56 * 1024 * 1024

F32 = jnp.float32
BF16 = jnp.bfloat16


def _sigmoid(x):
    return 0.5 * jnp.tanh(0.5 * x) + 0.5


def _params(sem):
    return pltpu.CompilerParams(dimension_semantics=sem, vmem_limit_bytes=VMEM_LIMIT)


def _mod_kernel(c_ref, w_ref, b_ref, o_ref):
    c = c_ref[...]
    sc = c * _sigmoid(c)
    o_ref[...] = jnp.dot(sc, w_ref[...], precision=lax.Precision.HIGHEST,
                         preferred_element_type=F32) + b_ref[...]


def _mod_call(c_pad, w_ada, b_ada):
    rows = c_pad.shape[0]
    n = w_ada.shape[1]
    tn = 1024
    return pl.pallas_call(
        _mod_kernel,
        grid=(n // tn,),
        in_specs=[pl.BlockSpec((rows, D_MODEL), lambda j: (0, 0)),
                  pl.BlockSpec((D_MODEL, tn), lambda j: (0, j)),
                  pl.BlockSpec((1, tn), lambda j: (0, j))],
        out_specs=pl.BlockSpec((rows, tn), lambda j: (0, j)),
        out_shape=jax.ShapeDtypeStruct((rows, n), F32),
        compiler_params=_params(("arbitrary",)),
        name="adaln_mod",
    )(c_pad, w_ada, b_ada.reshape(1, n))


def _inproj_kernel(x_ref, sc_ref, sh_ref, g_ref, cos_ref, sin_ref, w_ref, ugg_ref, qt_ref, kv_ref,
                   h_ref):
    j = pl.program_id(2)

    @pl.when(j == 0)
    def _():
        x = x_ref[...]
        ms = jnp.sum(x * x, axis=-1, keepdims=True) * (1.0 / D_MODEL)
        y = x * lax.rsqrt(ms + EPS) * g_ref[...]
        h_ref[...] = (y * (1.0 + sc_ref[...]) + sh_ref[...]).astype(BF16)

        lane = lax.broadcasted_iota(jnp.int32, (1, LANES), 1)
        first_half = (lane % HEAD_DIM) < (HEAD_DIM // 2)
        cos = cos_ref[...]
        sin = sin_ref[...]

        def rope(xc):
            partner = jnp.where(first_half, pltpu.roll(xc, LANES - HEAD_DIM // 2, 1),
                                pltpu.roll(xc, HEAD_DIM // 2, 1))
            return xc * cos + partner * sin

        h = h_ref[...]
        for c in range(QKV_TILE // MXU_N):
            acc = jnp.dot(h, w_ref[:, c * MXU_N:(c + 1) * MXU_N], preferred_element_type=F32)
            for half in range(MXU_N // LANES):
                col = c * MXU_N + half * LANES
                xc = acc[:, half * LANES:(half + 1) * LANES]
                if col < Q_DIM:
                    yq = rope(xc) * (HEAD_DIM ** -0.5)
                    qt_ref[col:col + LANES, :] = yq.T.astype(BF16)
                elif col < Q_DIM + KV_DIM:
                    kv_ref[:, col - Q_DIM:col - Q_DIM + LANES] = rope(xc).astype(BF16)
                else:
                    kv_ref[:, col - Q_DIM:col - Q_DIM + LANES] = xc.astype(BF16)

    @pl.when(j > 0)
    def _():
        ugg_ref[...] = jnp.dot(h_ref[...], w_ref[...], preferred_element_type=F32).astype(BF16)


def _inproj_call(x, mod, g_mix, cos_t, sin_t, w_in, tm):
    B, S, _ = x.shape
    tn = QKV_TILE
    n_tiles = IN_DIM // tn
    vec = lambda k: pl.BlockSpec((None, None, 1, D_MODEL), lambda b, i, j: (b, k, 0, 0))
    return pl.pallas_call(
        _inproj_kernel,
        grid=(B, S // tm, n_tiles),
        in_specs=[pl.BlockSpec((None, tm, D_MODEL), lambda b, i, j: (b, i, 0)),
                  vec(1), vec(0),
                  pl.BlockSpec((1, D_MODEL), lambda b, i, j: (0, 0)),
                  pl.BlockSpec((tm, LANES), lambda b, i, j: (i, 0)),
                  pl.BlockSpec((tm, LANES), lambda b, i, j: (i, 0)),
                  pl.BlockSpec((D_MODEL, tn), lambda b, i, j: (0, j))],
        out_specs=[pl.BlockSpec((None, tm, tn), lambda b, i, j: (b, i, jnp.maximum(j - 1, 0))),
                   pl.BlockSpec((None, Q_DIM, tm), lambda b, i, j: (b, 0, i)),
                   pl.BlockSpec((None, tm, 2 * KV_DIM), lambda b, i, j: (b, i, 0))],
        out_shape=[jax.ShapeDtypeStruct((B, S, IN_DIM - tn), BF16),
                   jax.ShapeDtypeStruct((B, Q_DIM, S), BF16),
                   jax.ShapeDtypeStruct((B, S, 2 * KV_DIM), BF16)],
        scratch_shapes=[pltpu.VMEM((tm, D_MODEL), BF16)],
        compiler_params=_params(("arbitrary", "arbitrary", "arbitrary")),
        name="inproj",
    )(x, mod, mod, g_mix, cos_t, sin_t, w_in)


def _attn_kernel(sink_ref, qt_ref, kvp_ref, kvm_ref, kvn_ref, o_ref, kvbuf, *, tq, seq):
    i = pl.program_id(1)
    nq = tq // BLOCK
    span = 3 * BLOCK
    cols = GROUP * BLOCK

    kvbuf[0:BLOCK] = kvp_ref[...]
    kvbuf[BLOCK:BLOCK + tq] = kvm_ref[...]
    kvbuf[BLOCK + tq:] = kvn_ref[...]

    kj = lax.broadcasted_iota(jnp.int32, (span, cols), 0)
    qi = lax.broadcasted_iota(jnp.int32, (span, cols), 1) % BLOCK
    neg = jnp.float32(-jnp.inf)
    in_window = jnp.where(jnp.abs(kj - BLOCK - qi) <= WINDOW, 0.0, neg)
    last_blk = seq // BLOCK - 1
    tn_dims = (((0,), (0,)), ((), ()))

    for qb in range(nq):
        r0 = qb * BLOCK
        blk = i * nq + qb
        bias = jnp.concatenate(
            [in_window[0:BLOCK] + jnp.where(blk > 0, 0.0, neg),
             in_window[BLOCK:2 * BLOCK],
             in_window[2 * BLOCK:] + jnp.where(blk < last_blk, 0.0, neg)], axis=0)
        for kh in range(N_KV_HEADS):
            qt = qt_ref[kh * Q_ROWS:(kh + 1) * Q_ROWS, r0:r0 + BLOCK]
            qt4 = jnp.concatenate([qt[g * HEAD_DIM:(g + 1) * HEAD_DIM] for g in range(GROUP)], axis=1)
            k_all = kvbuf[r0:r0 + span, kh * HEAD_DIM:(kh + 1) * HEAD_DIM]
            v_all = kvbuf[r0:r0 + span, KV_DIM + kh * HEAD_DIM:KV_DIM + (kh + 1) * HEAD_DIM]
            s = jnp.dot(k_all, qt4, preferred_element_type=F32) + bias
            sink = jnp.concatenate(
                [jnp.full((1, BLOCK), sink_ref[kh * GROUP + g], F32) for g in range(GROUP)], axis=1)
            m = jnp.maximum(jnp.max(s, axis=0, keepdims=True), sink)
            p = jnp.exp(s - m)
            den = jnp.sum(p, axis=0, keepdims=True) + jnp.exp(sink - m)
            ot = lax.dot_general(v_all, p.astype(BF16), tn_dims, preferred_element_type=F32)
            ot = ot * (1.0 / den)
            ot4 = jnp.concatenate([ot[:, g * BLOCK:(g + 1) * BLOCK] for g in range(GROUP)], axis=0)
            o_ref[r0:r0 + BLOCK, kh * Q_ROWS:(kh + 1) * Q_ROWS] = ot4.T.astype(BF16)


def _attn_call(qt, kv, sink, tq):
    B, S, _ = kv.shape
    nq = tq // BLOCK
    nb = S // BLOCK
    return pl.pallas_call(
        functools.partial(_attn_kernel, tq=tq, seq=S),
        grid=(B, S // tq),
        in_specs=[pl.BlockSpec(memory_space=pltpu.SMEM),
                  pl.BlockSpec((None, Q_DIM, tq), lambda b, i: (b, 0, i)),
                  pl.BlockSpec((None, BLOCK, 2 * KV_DIM),
                               lambda b, i: (b, jnp.maximum(i * nq - 1, 0), 0)),
                  pl.BlockSpec((None, tq, 2 * KV_DIM), lambda b, i: (b, i, 0)),
                  pl.BlockSpec((None, BLOCK, 2 * KV_DIM),
                               lambda b, i: (b, jnp.minimum((i + 1) * nq, nb - 1), 0))],
        out_specs=pl.BlockSpec((None, tq, Q_DIM), lambda b, i: (b, i, 0)),
        out_shape=jax.ShapeDtypeStruct((B, S, Q_DIM), BF16),
        scratch_shapes=[pltpu.VMEM((tq + 2 * BLOCK, 2 * KV_DIM), BF16)],
        compiler_params=_params(("arbitrary", "arbitrary")),
        name="band_attn",
    )(sink, qt, kv, kv, kv)


def _mixer_kernel(x_ref, a_ref, g_ref, ap_ref, gp_ref, an_ref, gn_ref, ga_ref, gc_ref, attn_ref,
                  gt1_ref, sc2_ref, sh2_ref, cw_ref, cb_ref, lg_ref, lb_ref, gf_ref,
                  wao_ref, wco_ref, wout_ref, x1_ref, h2_ref, hbuf, cbuf, *, tm):
    i = pl.program_id(1)
    last = pl.num_programs(1) - 1

    attn_o = jnp.dot(attn_ref[...], wao_ref[...], preferred_element_type=F32)

    def glu(a, g):
        return a.astype(F32) * _sigmoid(g.astype(F32))

    zero_halo = jnp.zeros((HALO, CONV_DIM), F32)
    hbuf[0:HALO] = jnp.where(i > 0, glu(ap_ref[...], gp_ref[...]), zero_halo)
    hbuf[HALO:HALO + tm] = glu(a_ref[...], g_ref[...])
    hbuf[HALO + tm:] = jnp.where(i < last, glu(an_ref[...], gn_ref[...]), zero_halo)

    rc = 64
    span = rc + 2 * HALO
    for r0 in range(0, tm, rc):
        for c in range(CONV_DIM // LANES):
            cs = slice(c * LANES, (c + 1) * LANES)
            col = hbuf[r0:r0 + span, cs]
            acc = jnp.zeros((rc, LANES), F32)
            for res in range(SUBLANES):
                rolled = col if res == 0 else pltpu.roll(col, span - res, 0)
                for q in range(2 * HALO // SUBLANES):
                    k = q * SUBLANES + res - (HALO - CONV_PAD)
                    if 0 <= k < CONV_K:
                        acc = acc + rolled[q * SUBLANES:q * SUBLANES + rc] * cw_ref[k:k + 1, cs]
            cbuf[r0:r0 + rc, cs] = acc

    cv = cbuf[...] + cb_ref[...]
    mu = jnp.sum(cv, axis=-1, keepdims=True) * (1.0 / CONV_DIM)
    d = cv - mu
    var = jnp.sum(d * d, axis=-1, keepdims=True) * (1.0 / CONV_DIM)
    ln = d * lax.rsqrt(var + EPS) * lg_ref[...] + lb_ref[...]
    act = (ln * _sigmoid(ln)).astype(BF16)

    conv_o = jnp.dot(act, wco_ref[...], preferred_element_type=F32)
    mix = (_sigmoid(ga_ref[...].astype(F32)) * attn_o
           + _sigmoid(gc_ref[...].astype(F32)) * conv_o).astype(BF16)
    y = jnp.dot(mix, wout_ref[...], preferred_element_type=F32)
    x1 = x_ref[...] + gt1_ref[...] * y
    x1_ref[...] = x1
    ms = jnp.sum(x1 * x1, axis=-1, keepdims=True) * (1.0 / D_MODEL)
    n = x1 * lax.rsqrt(ms + EPS) * gf_ref[...]
    h2_ref[...] = (n * (1.0 + sc2_ref[...]) + sh2_ref[...]).astype(BF16)


def _mixer_call(x, ugg, attn, mod, conv_w, conv_b, ln_g, ln_b, g_ffn, wao, wco, wout, tm):
    B, S, _ = x.shape
    hb = tm // HALO
    nh = S // HALO
    row = lambda w, col: pl.BlockSpec((None, tm, w), lambda b, i: (b, i, col))
    prev = lambda col: pl.BlockSpec((None, HALO, CONV_DIM),
                                    lambda b, i: (b, jnp.maximum(i * hb - 1, 0), col))
    nxt = lambda col: pl.BlockSpec((None, HALO, CONV_DIM),
                                   lambda b, i: (b, jnp.minimum((i + 1) * hb, nh - 1), col))
    vec = lambda k: pl.BlockSpec((None, None, 1, D_MODEL), lambda b, i: (b, k, 0, 0))
    const = lambda shape: pl.BlockSpec(shape, lambda b, i: (0, 0), pipeline_mode=pl.Buffered(1))
    return pl.pallas_call(
        functools.partial(_mixer_kernel, tm=tm),
        grid=(B, S // tm),
        in_specs=[row(D_MODEL, 0), row(CONV_DIM, 0), row(CONV_DIM, 1),
                  prev(0), prev(1), nxt(0), nxt(1),
                  row(D_MODEL, 1), row(D_MODEL, 2), row(Q_DIM, 0),
                  vec(2), vec(4), vec(3),
                  const((CONV_K + 1, CONV_DIM)), const((1, CONV_DIM)), const((1, CONV_DIM)),
                  const((1, CONV_DIM)), const((1, D_MODEL)),
                  const((Q_DIM, D_MODEL)), const((CONV_DIM, D_MODEL)), const((D_MODEL, D_MODEL))],
        out_specs=[row(D_MODEL, 0), row(D_MODEL, 0)],
        out_shape=[jax.ShapeDtypeStruct((B, S, D_MODEL), F32),
                   jax.ShapeDtypeStruct((B, S, D_MODEL), BF16)],
        scratch_shapes=[pltpu.VMEM((tm + 2 * HALO, CONV_DIM), F32),
                        pltpu.VMEM((tm, CONV_DIM), F32)],
        compiler_params=_params(("arbitrary", "arbitrary")),
        name="mixer_tail",
    )(x, ugg, ugg, ugg, ugg, ugg, ugg, ugg, ugg, attn, mod, mod, mod,
      conv_w, conv_b, ln_g, ln_b, g_ffn, wao, wco, wout)


def _ffn_kernel(h_ref, x1_ref, gt2_ref, gfin_ref, wa_ref, wb_ref, wo_ref, o_ref, *, final_norm):
    f = pl.program_id(2)

    @pl.when(f == 0)
    def _():
        o_ref[...] = x1_ref[...]

    h = h_ref[...]
    a = jnp.dot(h, wa_ref[...], preferred_element_type=F32)
    b = jnp.dot(h, wb_ref[...], preferred_element_type=F32)
    g = (a * _sigmoid(a) * b).astype(BF16)
    for n in range(D_MODEL // MXU_N):
        sl = slice(n * MXU_N, (n + 1) * MXU_N)
        o_ref[:, sl] += gt2_ref[:, sl] * jnp.dot(g, wo_ref[:, sl], preferred_element_type=F32)

    if final_norm:
        @pl.when(f == pl.num_programs(2) - 1)
        def _():
            x = o_ref[...]
            ms = jnp.sum(x * x, axis=-1, keepdims=True) * (1.0 / D_MODEL)
            o_ref[...] = x * lax.rsqrt(ms + EPS) * gfin_ref[...]


def _ffn_call(h2, x1, mod, g_final, w_ffn_in, w_ffn_out, tm, tf, final_norm):
    B, S, _ = x1.shape
    nf = D_FF // tf
    row = lambda: pl.BlockSpec((None, tm, D_MODEL), lambda b, i, f: (b, i, 0))
    return pl.pallas_call(
        functools.partial(_ffn_kernel, final_norm=final_norm),
        grid=(B, S // tm, nf),
        in_specs=[row(), row(),
                  pl.BlockSpec((None, None, 1, D_MODEL), lambda b, i, f: (b, 5, 0, 0)),
                  pl.BlockSpec((1, D_MODEL), lambda b, i, f: (0, 0)),
                  pl.BlockSpec((D_MODEL, tf), lambda b, i, f: (0, f)),
                  pl.BlockSpec((D_MODEL, tf), lambda b, i, f: (0, f + nf)),
                  pl.BlockSpec((tf, D_MODEL), lambda b, i, f: (f, 0))],
        out_specs=row(),
        out_shape=jax.ShapeDtypeStruct((B, S, D_MODEL), F32),
        compiler_params=_params(("arbitrary", "arbitrary", "arbitrary")),
        name="swiglu_ffn",
    )(h2, x1, mod, g_final, w_ffn_in, w_ffn_in, w_ffn_out)


def _rope_tables(seq):
    inv = 1.0 / (ROPE_THETA ** (jnp.arange(0, HEAD_DIM, 2, dtype=F32) / HEAD_DIM))
    ang = jnp.arange(seq, dtype=F32)[:, None] * inv[None, :]
    cos, sin = jnp.cos(ang), jnp.sin(ang)
    reps = LANES // HEAD_DIM
    cos_t = jnp.tile(jnp.concatenate([cos, cos], axis=-1), (1, reps))
    sin_t = jnp.tile(jnp.concatenate([-sin, sin], axis=-1), (1, reps))
    return cos_t, sin_t


def _encoder(x, mod, layers, g_final):
    B, S, _ = x.shape
    cos_t, sin_t = _rope_tables(S)
    for l, lw in enumerate(layers):
        m = mod[l]
        ugg, qt, kv = _inproj_call(x, m, lw["g_mix"], cos_t, sin_t, lw["w_in"], tm=512)
        attn = _attn_call(qt, kv, lw["sink"], tq=512)
        x1, h2 = _mixer_call(x, ugg, attn, m, lw["conv_w"], lw["conv_b"], lw["ln_g"], lw["ln_b"],
                             lw["g_ffn"], lw["wao"], lw["wco"], lw["wout"], tm=256)
        x = _ffn_call(h2, x1, m, g_final, lw["w_ffn_in"], lw["w_ffn_out"], tm=512, tf=512,
                      final_norm=(l == len(layers) - 1))
    return x


def kernel(x_prompt, x_sample, c_prompt, c_sample, w_ada, b_ada, g_mix, w_in, attn_sink, w_attn_o,
           conv_w, conv_b, conv_ln_g, conv_ln_b, w_conv_o, w_out, g_ffn, w_ffn_in, w_ffn_out, g_final):
    depth = w_in.shape[0]
    nb_p, nb_s = c_prompt.shape[0], c_sample.shape[0]
    rows = -(-(nb_p + nb_s) // 8) * 8
    c_all = jnp.concatenate([c_prompt, c_sample], axis=0)
    c_pad = jnp.pad(c_all, ((0, rows - nb_p - nb_s), (0, 0)))

    layers, mods_p, mods_s = [], [], []
    for l in range(depth):
        mod = _mod_call(c_pad, w_ada[l], b_ada[l]).reshape(rows, N_MOD, 1, D_MODEL)
        mods_p.append(mod[:nb_p])
        mods_s.append(mod[nb_p:nb_p + nb_s])
        layers.append(dict(
            g_mix=g_mix[l].reshape(1, D_MODEL), w_in=w_in[l].astype(BF16), sink=attn_sink[l],
            wao=w_attn_o[l].astype(BF16),
            conv_w=jnp.pad(conv_w[l], ((0, 1), (0, 0))), conv_b=conv_b[l].reshape(1, CONV_DIM),
            ln_g=conv_ln_g[l].reshape(1, CONV_DIM), ln_b=conv_ln_b[l].reshape(1, CONV_DIM),
            wco=w_conv_o[l].astype(BF16), wout=w_out[l].astype(BF16),
            g_ffn=g_ffn[l].reshape(1, D_MODEL), w_ffn_in=w_ffn_in[l].astype(BF16),
            w_ffn_out=w_ffn_out[l].astype(BF16)))
    gfin = g_final.reshape(1, D_MODEL)
    y_prompt = _encoder(x_prompt, mods_p, layers, gfin)
    y_sample = _encoder(x_sample, mods_s, layers, gfin)
    return (y_prompt, y_sample)
```

```python
import functools
import math

import jax
import jax.numpy as jnp
from jax import lax
from jax.experimental import pallas as pl
from jax.experimental.pallas import tpu as pltpu

D_MODEL = 2048
N_HEADS = 16
N_KV_HEADS = 4
GROUP = N_HEADS // N_KV_HEADS
HEAD_DIM = 64
Q_DIM = N_HEADS * HEAD_DIM
KV_DIM = N_KV_HEADS * HEAD_DIM
BLOCK = 128
WINDOW = 128
ROPE_THETA = 10000.0
CONV_DIM = 1024
CONV_K = 31
CONV_PAD = CONV_K // 2
D_FF = int(math.ceil(8 * D_MODEL / 3 / 256) * 256)
N_MOD = 6
EPS = 1e-6
IN_DIM = Q_DIM + 2 * KV_DIM + 2 * CONV_DIM + 2 * D_MODEL

LANES = 128
SUBLANES = 8
MXU_N = 256
HALO = 16
QKV_TILE = Q_DIM + 2 * KV_DIM
Q_ROWS = GROUP * HEAD_DIM
LOG2E = 1.4426950408889634
Q_SCALE = HEAD_DIM ** -0.5 * LOG2E

VMEM_LIMIT = 56 * 1024 * 1024

F32 = jnp.float32
BF16 = jnp.bfloat16


def _sigmoid(x):
    return 0.5 * jnp.tanh(0.5 * x) + 0.5


def _params(sem):
    return pltpu.CompilerParams(dimension_semantics=sem, vmem_limit_bytes=VMEM_LIMIT)


def _mod_kernel(c_ref, w_ref, b_ref, o_ref):
    c = c_ref[...]
    sc = c * _sigmoid(c)
    o_ref[...] = jnp.dot(sc, w_ref[...], precision=lax.Precision.HIGHEST,
                         preferred_element_type=F32) + b_ref[...]


def _mod_call(c_pad, w_ada, b_ada):
    rows = c_pad.shape[0]
    n = w_ada.shape[1]
    tn = 1024
    return pl.pallas_call(
        _mod_kernel,
        grid=(n // tn,),
        in_specs=[pl.BlockSpec((rows, D_MODEL), lambda j: (0, 0)),
                  pl.BlockSpec((D_MODEL, tn), lambda j: (0, j)),
                  pl.BlockSpec((1, tn), lambda j: (0, j))],
        out_specs=pl.BlockSpec((rows, tn), lambda j: (0, j)),
        out_shape=jax.ShapeDtypeStruct((rows, n), F32),
        compiler_params=_params(("arbitrary",)),
        name="adaln_mod",
    )(c_pad, w_ada, b_ada.reshape(1, n))


def _inproj_kernel(x_ref, sc_ref, sh_ref, g_ref, cos_ref, sin_ref, w_ref, ugg_ref, qt_ref, kv_ref,
                   h_ref):
    j = pl.program_id(2)

    @pl.when(j == 0)
    def _():
        x = x_ref[...]
        ms = jnp.sum(x * x, axis=-1, keepdims=True) * (1.0 / D_MODEL)
        y = x * lax.rsqrt(ms + EPS) * g_ref[...]
        h_ref[...] = (y * (1.0 + sc_ref[...]) + sh_ref[...]).astype(BF16)

        lane = lax.broadcasted_iota(jnp.int32, (1, LANES), 1)
        first_half = (lane % HEAD_DIM) < (HEAD_DIM // 2)
        cos = cos_ref[...]
        sin = sin_ref[...]

        def rope(xc):
            partner = jnp.where(first_half, pltpu.roll(xc, LANES - HEAD_DIM // 2, 1),
                                pltpu.roll(xc, HEAD_DIM // 2, 1))
            return xc * cos + partner * sin

        h = h_ref[...]
        for c in range(QKV_TILE // MXU_N):
            acc = jnp.dot(h, w_ref[:, c * MXU_N:(c + 1) * MXU_N], preferred_element_type=F32)
            for half in range(MXU_N // LANES):
                col = c * MXU_N + half * LANES
                xc = acc[:, half * LANES:(half + 1) * LANES]
                if col < Q_DIM:
                    yq = rope(xc) * Q_SCALE
                    qt_ref[col:col + LANES, :] = yq.T.astype(BF16)
                elif col < Q_DIM + KV_DIM:
                    kv_ref[:, col - Q_DIM:col - Q_DIM + LANES] = rope(xc).astype(BF16)
                else:
                    kv_ref[:, col - Q_DIM:col - Q_DIM + LANES] = xc.astype(BF16)

    @pl.when(j > 0)
    def _():
        ugg_ref[...] = jnp.dot(h_ref[...], w_ref[...], preferred_element_type=F32).astype(BF16)


def _inproj_call(x, mod, g_mix, cos_t, sin_t, w_in, tm):
    B, S, _ = x.shape
    tn = QKV_TILE
    n_tiles = IN_DIM // tn
    vec = lambda k: pl.BlockSpec((None, None, 1, D_MODEL), lambda b, i, j: (b, k, 0, 0))
    return pl.pallas_call(
        _inproj_kernel,
        grid=(B, S // tm, n_tiles),
        in_specs=[pl.BlockSpec((None, tm, D_MODEL), lambda b, i, j: (b, i, 0)),
                  vec(1), vec(0),
                  pl.BlockSpec((1, D_MODEL), lambda b, i, j: (0, 0)),
                  pl.BlockSpec((tm, LANES), lambda b, i, j: (i, 0)),
                  pl.BlockSpec((tm, LANES), lambda b, i, j: (i, 0)),
                  pl.BlockSpec((D_MODEL, tn), lambda b, i, j: (0, j))],
        out_specs=[pl.BlockSpec((None, tm, tn), lambda b, i, j: (b, i, jnp.maximum(j - 1, 0))),
                   pl.BlockSpec((None, Q_DIM, tm), lambda b, i, j: (b, 0, i)),
                   pl.BlockSpec((None, tm, 2 * KV_DIM), lambda b, i, j: (b, i, 0))],
        out_shape=[jax.ShapeDtypeStruct((B, S, IN_DIM - tn), BF16),
                   jax.ShapeDtypeStruct((B, Q_DIM, S), BF16),
                   jax.ShapeDtypeStruct((B, S, 2 * KV_DIM), BF16)],
        scratch_shapes=[pltpu.VMEM((tm, D_MODEL), BF16)],
        compiler_params=_params(("arbitrary", "arbitrary", "arbitrary")),
        name="inproj",
    )(x, mod, mod, g_mix, cos_t, sin_t, w_in)


def _attn_kernel(sink_ref, qt_ref, kvp_ref, kvm_ref, kvn_ref, o_ref, kvbuf, *, tq, seq):
    i = pl.program_id(1)
    nq = tq // BLOCK
    span = 3 * BLOCK
    cols = GROUP * BLOCK

    kvbuf[0:BLOCK] = kvp_ref[...]
    kvbuf[BLOCK:BLOCK + tq] = kvm_ref[...]
    kvbuf[BLOCK + tq:] = kvn_ref[...]

    kj = lax.broadcasted_iota(jnp.int32, (span, cols), 0)
    qi = lax.broadcasted_iota(jnp.int32, (span, cols), 1) % BLOCK
    neg = jnp.float32(-jnp.inf)
    in_window = jnp.where(jnp.abs(kj - BLOCK - qi) <= WINDOW, 0.0, neg)
    last_blk = seq // BLOCK - 1
    tn_dims = (((0,), (0,)), ((), ()))

    for qb in range(nq):
        r0 = qb * BLOCK
        blk = i * nq + qb
        bias = jnp.concatenate(
            [in_window[0:BLOCK] + jnp.where(blk > 0, 0.0, neg),
             in_window[BLOCK:2 * BLOCK],
             in_window[2 * BLOCK:] + jnp.where(blk < last_blk, 0.0, neg)], axis=0)
        for kh in range(N_KV_HEADS):
            qt = qt_ref[kh * Q_ROWS:(kh + 1) * Q_ROWS, r0:r0 + BLOCK]
            qt4 = jnp.concatenate([qt[g * HEAD_DIM:(g + 1) * HEAD_DIM] for g in range(GROUP)], axis=1)
            k_all = kvbuf[r0:r0 + span, kh * HEAD_DIM:(kh + 1) * HEAD_DIM]
            v_all = kvbuf[r0:r0 + span, KV_DIM + kh * HEAD_DIM:KV_DIM + (kh + 1) * HEAD_DIM]
            s = jnp.dot(k_all, qt4, preferred_element_type=F32) + bias
            sink = jnp.concatenate(
                [jnp.full((1, BLOCK), sink_ref[kh * GROUP + g] * LOG2E, F32) for g in range(GROUP)],
                axis=1)
            m = jnp.maximum(jnp.max(s, axis=0, keepdims=True), sink)
            p = jnp.exp2(s - m)
            den = jnp.sum(p, axis=0, keepdims=True) + jnp.exp2(sink - m)
            ot = lax.dot_general(v_all, p.astype(BF16), tn_dims, preferred_element_type=F32)
            ot = ot * (1.0 / den)
            ot4 = jnp.concatenate([ot[:, g * BLOCK:(g + 1) * BLOCK] for g in range(GROUP)], axis=0)
            o_ref[r0:r0 + BLOCK, kh * Q_ROWS:(kh + 1) * Q_ROWS] = ot4.T.astype(BF16)


def _attn_call(qt, kv, sink, tq):
    B, S, _ = kv.shape
    nq = tq // BLOCK
    nb = S // BLOCK
    return pl.pallas_call(
        functools.partial(_attn_kernel, tq=tq, seq=S),
        grid=(B, S // tq),
        in_specs=[pl.BlockSpec(memory_space=pltpu.SMEM),
                  pl.BlockSpec((None, Q_DIM, tq), lambda b, i: (b, 0, i)),
                  pl.BlockSpec((None, BLOCK, 2 * KV_DIM),
                               lambda b, i: (b, jnp.maximum(i * nq - 1, 0), 0)),
                  pl.BlockSpec((None, tq, 2 * KV_DIM), lambda b, i: (b, i, 0)),
                  pl.BlockSpec((None, BLOCK, 2 * KV_DIM),
                               lambda b, i: (b, jnp.minimum((i + 1) * nq, nb - 1), 0))],
        out_specs=pl.BlockSpec((None, tq, Q_DIM), lambda b, i: (b, i, 0)),
        out_shape=jax.ShapeDtypeStruct((B, S, Q_DIM), BF16),
        scratch_shapes=[pltpu.VMEM((tq + 2 * BLOCK, 2 * KV_DIM), BF16)],
        compiler_params=_params(("arbitrary", "arbitrary")),
        name="band_attn",
    )(sink, qt, kv, kv, kv)


def _mixer_kernel(x_ref, a_ref, g_ref, ap_ref, gp_ref, an_ref, gn_ref, ga_ref, gc_ref, attn_ref,
                  gt1_ref, sc2_ref, sh2_ref, cw_ref, cb_ref, lg_ref, lb_ref, gf_ref,
                  wao_ref, wco_ref, wout_ref, x1_ref, h2_ref, hbuf, cbuf, *, tm):
    i = pl.program_id(1)
    last = pl.num_programs(1) - 1

    attn_o = jnp.dot(attn_ref[...], wao_ref[...], preferred_element_type=F32)

    def glu(a, g):
        return a.astype(F32) * _sigmoid(g.astype(F32))

    zero_halo = jnp.zeros((HALO, CONV_DIM), F32)
    hbuf[0:HALO] = jnp.where(i > 0, glu(ap_ref[...], gp_ref[...]), zero_halo)
    hbuf[HALO:HALO + tm] = glu(a_ref[...], g_ref[...])
    hbuf[HALO + tm:] = jnp.where(i < last, glu(an_ref[...], gn_ref[...]), zero_halo)

    rc = 64
    span = rc + 2 * HALO
    for r0 in range(0, tm, rc):
        for c in range(CONV_DIM // LANES):
            cs = slice(c * LANES, (c + 1) * LANES)
            col = hbuf[r0:r0 + span, cs]
            acc = jnp.zeros((rc, LANES), F32)
            for res in range(SUBLANES):
                rolled = col if res == 0 else pltpu.roll(col, span - res, 0)
                for q in range(2 * HALO // SUBLANES):
                    k = q * SUBLANES + res - (HALO - CONV_PAD)
                    if 0 <= k < CONV_K:
                        acc = acc + rolled[q * SUBLANES:q * SUBLANES + rc] * cw_ref[k:k + 1, cs]
            cbuf[r0:r0 + rc, cs] = acc

    cv = cbuf[...] + cb_ref[...]
    mu = jnp.sum(cv, axis=-1, keepdims=True) * (1.0 / CONV_DIM)
    d = cv - mu
    var = jnp.sum(d * d, axis=-1, keepdims=True) * (1.0 / CONV_DIM)
    ln = d * lax.rsqrt(var + EPS) * lg_ref[...] + lb_ref[...]
    act = (ln * _sigmoid(ln)).astype(BF16)

    conv_o = jnp.dot(act, wco_ref[...], preferred_element_type=F32)
    mix = (_sigmoid(ga_ref[...].astype(F32)) * attn_o
           + _sigmoid(gc_ref[...].astype(F32)) * conv_o).astype(BF16)
    y = jnp.dot(mix, wout_ref[...], preferred_element_type=F32)
    x1 = x_ref[...] + gt1_ref[...] * y
    x1_ref[...] = x1
    ms = jnp.sum(x1 * x1, axis=-1, keepdims=True) * (1.0 / D_MODEL)
    n = x1 * lax.rsqrt(ms + EPS) * gf_ref[...]
    h2_ref[...] = (n * (1.0 + sc2_ref[...]) + sh2_ref[...]).astype(BF16)


def _mixer_call(x, ugg, attn, mod, conv_w, conv_b, ln_g, ln_b, g_ffn, wao, wco, wout, tm):
    B, S, _ = x.shape
    hb = tm // HALO
    nh = S // HALO
    row = lambda w, col: pl.BlockSpec((None, tm, w), lambda b, i: (b, i, col))
    prev = lambda col: pl.BlockSpec((None, HALO, CONV_DIM),
                                    lambda b, i: (b, jnp.maximum(i * hb - 1, 0), col))
    nxt = lambda col: pl.BlockSpec((None, HALO, CONV_DIM),
                                   lambda b, i: (b, jnp.minimum((i + 1) * hb, nh - 1), col))
    vec = lambda k: pl.BlockSpec((None, None, 1, D_MODEL), lambda b, i: (b, k, 0, 0))
    const = lambda shape: pl.BlockSpec(shape, lambda b, i: (0, 0), pipeline_mode=pl.Buffered(1))
    return pl.pallas_call(
        functools.partial(_mixer_kernel, tm=tm),
        grid=(B, S // tm),
        in_specs=[row(D_MODEL, 0), row(CONV_DIM, 0), row(CONV_DIM, 1),
                  prev(0), prev(1), nxt(0), nxt(1),
                  row(D_MODEL, 1), row(D_MODEL, 2), row(Q_DIM, 0),
                  vec(2), vec(4), vec(3),
                  const((CONV_K + 1, CONV_DIM)), const((1, CONV_DIM)), const((1, CONV_DIM)),
                  const((1, CONV_DIM)), const((1, D_MODEL)),
                  const((Q_DIM, D_MODEL)), const((CONV_DIM, D_MODEL)), const((D_MODEL, D_MODEL))],
        out_specs=[row(D_MODEL, 0), row(D_MODEL, 0)],
        out_shape=[jax.ShapeDtypeStruct((B, S, D_MODEL), F32),
                   jax.ShapeDtypeStruct((B, S, D_MODEL), BF16)],
        scratch_shapes=[pltpu.VMEM((tm + 2 * HALO, CONV_DIM), F32),
                        pltpu.VMEM((tm, CONV_DIM), F32)],
        compiler_params=_params(("arbitrary", "arbitrary")),
        name="mixer_tail",
    )(x, ugg, ugg, ugg, ugg, ugg, ugg, ugg, ugg, attn, mod, mod, mod,
      conv_w, conv_b, ln_g, ln_b, g_ffn, wao, wco, wout)


def _ffn_kernel(h_ref, x1_ref, gt2_ref, gfin_ref, wa_ref, wb_ref, wo_ref, o_ref, *, final_norm):
    f = pl.program_id(2)

    @pl.when(f == 0)
    def _():
        o_ref[...] = x1_ref[...]

    h = h_ref[...]
    a = jnp.dot(h, wa_ref[...], preferred_element_type=F32)
    b = jnp.dot(h, wb_ref[...], preferred_element_type=F32)
    g = (a * _sigmoid(a) * b).astype(BF16)
    for n in range(D_MODEL // MXU_N):
        sl = slice(n * MXU_N, (n + 1) * MXU_N)
        o_ref[:, sl] += gt2_ref[:, sl] * jnp.dot(g, wo_ref[:, sl], preferred_element_type=F32)

    if final_norm:
        @pl.when(f == pl.num_programs(2) - 1)
        def _():
            x = o_ref[...]
            ms = jnp.sum(x * x, axis=-1, keepdims=True) * (1.0 / D_MODEL)
            o_ref[...] = x * lax.rsqrt(ms + EPS) * gfin_ref[...]


def _ffn_call(h2, x1, mod, g_final, w_ffn_in, w_ffn_out, tm, tf, final_norm):
    B, S, _ = x1.shape
    nf = D_FF // tf
    row = lambda: pl.BlockSpec((None, tm, D_MODEL), lambda b, i, f: (b, i, 0))
    x1_spec = pl.BlockSpec((None, tm, D_MODEL), lambda b, i, f: (b, i, 0),
                           pipeline_mode=pl.Buffered(1))
    return pl.pallas_call(
        functools.partial(_ffn_kernel, final_norm=final_norm),
        grid=(B, S // tm, nf),
        in_specs=[row(), x1_spec,
                  pl.BlockSpec((None, None, 1, D_MODEL), lambda b, i, f: (b, 5, 0, 0)),
                  pl.BlockSpec((1, D_MODEL), lambda b, i, f: (0, 0)),
                  pl.BlockSpec((D_MODEL, tf), lambda b, i, f: (0, f)),
                  pl.BlockSpec((D_MODEL, tf), lambda b, i, f: (0, f + nf)),
                  pl.BlockSpec((tf, D_MODEL), lambda b, i, f: (f, 0))],
        out_specs=row(),
        out_shape=jax.ShapeDtypeStruct((B, S, D_MODEL), F32),
        compiler_params=_params(("arbitrary", "arbitrary", "arbitrary")),
        name="swiglu_ffn",
    )(h2, x1, mod, g_final, w_ffn_in, w_ffn_in, w_ffn_out)


def _rope_tables(seq):
    inv = 1.0 / (ROPE_THETA ** (jnp.arange(0, HEAD_DIM, 2, dtype=F32) / HEAD_DIM))
    ang = jnp.arange(seq, dtype=F32)[:, None] * inv[None, :]
    cos, sin = jnp.cos(ang), jnp.sin(ang)
    reps = LANES // HEAD_DIM
    cos_t = jnp.tile(jnp.concatenate([cos, cos], axis=-1), (1, reps))
    sin_t = jnp.tile(jnp.concatenate([-sin, sin], axis=-1), (1, reps))
    return cos_t, sin_t


def _encoder(x, mod, layers, g_final):
    B, S, _ = x.shape
    cos_t, sin_t = _rope_tables(S)
    for l, lw in enumerate(layers):
        m = mod[l]
        ugg, qt, kv = _inproj_call(x, m, lw["g_mix"], cos_t, sin_t, lw["w_in"], tm=1024)
        attn = _attn_call(qt, kv, lw["sink"], tq=512)
        x1, h2 = _mixer_call(x, ugg, attn, m, lw["conv_w"], lw["conv_b"], lw["ln_g"], lw["ln_b"],
                             lw["g_ffn"], lw["wao"], lw["wco"], lw["wout"], tm=256)
        x = _ffn_call(h2, x1, m, g_final, lw["w_ffn_in"], lw["w_ffn_out"], tm=1024, tf=512,
                      final_norm=(l == len(layers) - 1))
    return x


def kernel(x_prompt, x_sample, c_prompt, c_sample, w_ada, b_ada, g_mix, w_in, attn_sink, w_attn_o,
           conv_w, conv_b, conv_ln_g, conv_ln_b, w_conv_o, w_out, g_ffn, w_ffn_in, w_ffn_out, g_final):
    depth = w_in.shape[0]
    nb_p, nb_s = c_prompt.shape[0], c_sample.shape[0]
    rows = -(-(nb_p + nb_s) // 8) * 8
    c_all = jnp.concatenate([c_prompt, c_sample], axis=0)
    c_pad = jnp.pad(c_all, ((0, rows - nb_p - nb_s), (0, 0)))

    layers, mods_p, mods_s = [], [], []
    for l in range(depth):
        mod = _mod_call(c_pad, w_ada[l], b_ada[l]).reshape(rows, N_MOD, 1, D_MODEL)
        mods_p.append(mod[:nb_p])
        mods_s.append(mod[nb_p:nb_p + nb_s])
        layers.append(dict(
            g_mix=g_mix[l].reshape(1, D_MODEL), w_in=w_in[l].astype(BF16), sink=attn_sink[l],
            wao=w_attn_o[l].astype(BF16),
            conv_w=jnp.pad(conv_w[l], ((0, 1), (0, 0))), conv_b=conv_b[l].reshape(1, CONV_DIM),
            ln_g=conv_ln_g[l].reshape(1, CONV_DIM), ln_b=conv_ln_b[l].reshape(1, CONV_DIM),
            wco=w_conv_o[l].astype(BF16), wout=w_out[l].astype(BF16),
            g_ffn=g_ffn[l].reshape(1, D_MODEL), w_ffn_in=w_ffn_in[l].astype(BF16),
            w_ffn_out=w_ffn_out[l].astype(BF16)))
    gfin = g_final.reshape(1, D_MODEL)
    y_prompt = _encoder(x_prompt, mods_p, layers, gfin)
    y_sample = _encoder(x_sample, mods_s, layers, gfin)
    return (y_prompt, y_sample)
```

```python
import functools
import math

import jax
import jax.numpy as jnp
from jax import lax
from jax.experimental import pallas as pl
from jax.experimental.pallas import tpu as pltpu

D_MODEL = 2048
N_HEADS = 16
N_KV_HEADS = 4
GROUP = N_HEADS // N_KV_HEADS
HEAD_DIM = 64
Q_DIM = N_HEADS * HEAD_DIM
KV_DIM = N_KV_HEADS * HEAD_DIM
BLOCK = 128
WINDOW = 128
ROPE_THETA = 10000.0
CONV_DIM = 1024
CONV_K = 31
CONV_PAD = CONV_K // 2
D_FF = int(math.ceil(8 * D_MODEL / 3 / 256) * 256)
N_MOD = 6
EPS = 1e-6
IN_DIM = Q_DIM + 2 * KV_DIM + 2 * CONV_DIM + 2 * D_MODEL

LANES = 128
SUBLANES = 8
MXU_N = 256
HALO = 16
QKV_TILE = Q_DIM + 2 * KV_DIM
Q_ROWS = GROUP * HEAD_DIM
LOG2E = 1.4426950408889634
Q_SCALE = HEAD_DIM ** -0.5 * LOG2E

VMEM_LIMIT = 56 * 1024 * 1024

F32 = jnp.float32
BF16 = jnp.bfloat16


def _sigmoid(x):
    return 0.5 * jnp.tanh(0.5 * x) + 0.5


def _params(sem):
    return pltpu.CompilerParams(dimension_semantics=sem, vmem_limit_bytes=VMEM_LIMIT)


def _mod_kernel(c_ref, w_ref, b_ref, o_ref):
    c = c_ref[...]
    sc = c * _sigmoid(c)
    o_ref[...] = jnp.dot(sc, w_ref[...], precision=lax.Precision.HIGHEST,
                         preferred_element_type=F32) + b_ref[...]


def _mod_call(c_pad, w_ada, b_ada):
    rows = c_pad.shape[0]
    n = w_ada.shape[1]
    tn = 1024
    return pl.pallas_call(
        _mod_kernel,
        grid=(n // tn,),
        in_specs=[pl.BlockSpec((rows, D_MODEL), lambda j: (0, 0)),
                  pl.BlockSpec((D_MODEL, tn), lambda j: (0, j)),
                  pl.BlockSpec((1, tn), lambda j: (0, j))],
        out_specs=pl.BlockSpec((rows, tn), lambda j: (0, j)),
        out_shape=jax.ShapeDtypeStruct((rows, n), F32),
        compiler_params=_params(("arbitrary",)),
        name="adaln_mod",
    )(c_pad, w_ada, b_ada.reshape(1, n))


def _inproj_kernel(x_ref, sc_ref, sh_ref, g_ref, cos_ref, sin_ref, w_ref, ugg_ref, qt_ref, kv_ref,
                   h_ref):
    j = pl.program_id(2)

    @pl.when(j == 0)
    def _():
        x = x_ref[...]
        ms = jnp.sum(x * x, axis=-1, keepdims=True) * (1.0 / D_MODEL)
        y = x * lax.rsqrt(ms + EPS) * g_ref[...]
        h_ref[...] = (y * (1.0 + sc_ref[...]) + sh_ref[...]).astype(BF16)

        lane = lax.broadcasted_iota(jnp.int32, (1, LANES), 1)
        first_half = (lane % HEAD_DIM) < (HEAD_DIM // 2)
        cos = cos_ref[...]
        sin = sin_ref[...]

        def rope(xc):
            partner = jnp.where(first_half, pltpu.roll(xc, LANES - HEAD_DIM // 2, 1),
                                pltpu.roll(xc, HEAD_DIM // 2, 1))
            return xc * cos + partner * sin

        h = h_ref[...]
        for c in range(QKV_TILE // MXU_N):
            acc = jnp.dot(h, w_ref[:, c * MXU_N:(c + 1) * MXU_N], preferred_element_type=F32)
            for half in range(MXU_N // LANES):
                col = c * MXU_N + half * LANES
                xc = acc[:, half * LANES:(half + 1) * LANES]
                if col < Q_DIM:
                    yq = rope(xc) * Q_SCALE
                    qt_ref[col:col + LANES, :] = yq.T.astype(BF16)
                elif col < Q_DIM + KV_DIM:
                    kv_ref[:, col - Q_DIM:col - Q_DIM + LANES] = rope(xc).astype(BF16)
                else:
                    kv_ref[:, col - Q_DIM:col - Q_DIM + LANES] = xc.astype(BF16)

    @pl.when(j > 0)
    def _():
        ugg_ref[...] = jnp.dot(h_ref[...], w_ref[...], preferred_element_type=F32).astype(BF16)


def _inproj_call(x, mod, g_mix, cos_t, sin_t, w_in, tm):
    B, S, _ = x.shape
    tn = QKV_TILE
    n_tiles = IN_DIM // tn
    vec = lambda k: pl.BlockSpec((None, None, 1, D_MODEL), lambda b, i, j: (b, k, 0, 0))
    return pl.pallas_call(
        _inproj_kernel,
        grid=(B, S // tm, n_tiles),
        in_specs=[pl.BlockSpec((None, tm, D_MODEL), lambda b, i, j: (b, i, 0)),
                  vec(1), vec(0),
                  pl.BlockSpec((1, D_MODEL), lambda b, i, j: (0, 0)),
                  pl.BlockSpec((tm, LANES), lambda b, i, j: (i, 0)),
                  pl.BlockSpec((tm, LANES), lambda b, i, j: (i, 0)),
                  pl.BlockSpec((D_MODEL, tn), lambda b, i, j: (0, j))],
        out_specs=[pl.BlockSpec((None, tm, tn), lambda b, i, j: (b, i, jnp.maximum(j - 1, 0))),
                   pl.BlockSpec((None, Q_DIM, tm), lambda b, i, j: (b, 0, i)),
                   pl.BlockSpec((None, tm, 2 * KV_DIM), lambda b, i, j: (b, i, 0))],
        out_shape=[jax.ShapeDtypeStruct((B, S, IN_DIM - tn), BF16),
                   jax.ShapeDtypeStruct((B, Q_DIM, S), BF16),
                   jax.ShapeDtypeStruct((B, S, 2 * KV_DIM), BF16)],
        scratch_shapes=[pltpu.VMEM((tm, D_MODEL), BF16)],
        compiler_params=_params(("arbitrary", "arbitrary", "arbitrary")),
        name="inproj",
    )(x, mod, mod, g_mix, cos_t, sin_t, w_in)


def _attn_kernel(sink_ref, qt_ref, kvp_ref, kvm_ref, kvn_ref, o_ref, kvbuf, *, tq, seq):
    i = pl.program_id(1)
    nq = tq // BLOCK
    span = 3 * BLOCK
    cols = GROUP * BLOCK

    kvbuf[0:BLOCK] = kvp_ref[...]
    kvbuf[BLOCK:BLOCK + tq] = kvm_ref[...]
    kvbuf[BLOCK + tq:] = kvn_ref[...]

    kj = lax.broadcasted_iota(jnp.int32, (span, cols), 0)
    qi = lax.broadcasted_iota(jnp.int32, (span, cols), 1) % BLOCK
    neg = jnp.float32(-jnp.inf)
    in_window = jnp.where(jnp.abs(kj - BLOCK - qi) <= WINDOW, 0.0, neg)
    last_blk = seq // BLOCK - 1
    tn_dims = (((0,), (0,)), ((), ()))

    for qb in range(nq):
        r0 = qb * BLOCK
        blk = i * nq + qb
        bias = jnp.concatenate(
            [in_window[0:BLOCK] + jnp.where(blk > 0, 0.0, neg),
             in_window[BLOCK:2 * BLOCK],
             in_window[2 * BLOCK:] + jnp.where(blk < last_blk, 0.0, neg)], axis=0)
        for kh in range(N_KV_HEADS):
            qt = qt_ref[kh * Q_ROWS:(kh + 1) * Q_ROWS, r0:r0 + BLOCK]
            qt4 = jnp.concatenate([qt[g * HEAD_DIM:(g + 1) * HEAD_DIM] for g in range(GROUP)], axis=1)
            k_all = kvbuf[r0:r0 + span, kh * HEAD_DIM:(kh + 1) * HEAD_DIM]
            v_all = kvbuf[r0:r0 + span, KV_DIM + kh * HEAD_DIM:KV_DIM + (kh + 1) * HEAD_DIM]
            s = jnp.dot(k_all, qt4, preferred_element_type=F32) + bias
            sink = jnp.concatenate(
                [jnp.full((1, BLOCK), sink_ref[kh * GROUP + g] * LOG2E, F32) for g in range(GROUP)],
                axis=1)
            m = jnp.maximum(jnp.max(s, axis=0, keepdims=True), sink)
            p = jnp.exp2(s - m)
            den = jnp.sum(p, axis=0, keepdims=True) + jnp.exp2(sink - m)
            ot = lax.dot_general(v_all, p.astype(BF16), tn_dims, preferred_element_type=F32)
            ot = ot * (1.0 / den)
            ot4 = jnp.concatenate([ot[:, g * BLOCK:(g + 1) * BLOCK] for g in range(GROUP)], axis=0)
            o_ref[r0:r0 + BLOCK, kh * Q_ROWS:(kh + 1) * Q_ROWS] = ot4.T.astype(BF16)


def _attn_call(qt, kv, sink, tq):
    B, S, _ = kv.shape
    nq = tq // BLOCK
    nb = S // BLOCK
    return pl.pallas_call(
        functools.partial(_attn_kernel, tq=tq, seq=S),
        grid=(B, S // tq),
        in_specs=[pl.BlockSpec(memory_space=pltpu.SMEM),
                  pl.BlockSpec((None, Q_DIM, tq), lambda b, i: (b, 0, i)),
                  pl.BlockSpec((None, BLOCK, 2 * KV_DIM),
                               lambda b, i: (b, jnp.maximum(i * nq - 1, 0), 0)),
                  pl.BlockSpec((None, tq, 2 * KV_DIM), lambda b, i: (b, i, 0)),
                  pl.BlockSpec((None, BLOCK, 2 * KV_DIM),
                               lambda b, i: (b, jnp.minimum((i + 1) * nq, nb - 1), 0))],
        out_specs=pl.BlockSpec((None, tq, Q_DIM), lambda b, i: (b, i, 0)),
        out_shape=jax.ShapeDtypeStruct((B, S, Q_DIM), BF16),
        scratch_shapes=[pltpu.VMEM((tq + 2 * BLOCK, 2 * KV_DIM), BF16)],
        compiler_params=_params(("arbitrary", "arbitrary")),
        name="band_attn",
    )(sink, qt, kv, kv, kv)


def _mixer_kernel(x_ref, a_ref, g_ref, ap_ref, gp_ref, an_ref, gn_ref, ga_ref, gc_ref, attn_ref,
                  gt1_ref, sc2_ref, sh2_ref, cw_ref, cb_ref, lg_ref, lb_ref, gf_ref,
                  wao_ref, wco_ref, wout_ref, x1_ref, h2_ref, hbuf, cbuf, *, tm):
    i = pl.program_id(1)
    last = pl.num_programs(1) - 1

    attn_o = jnp.dot(attn_ref[...], wao_ref[...], preferred_element_type=F32)

    def glu(a, g):
        return a.astype(F32) * _sigmoid(g.astype(F32))

    zero_halo = jnp.zeros((HALO, CONV_DIM), F32)
    hbuf[0:HALO] = jnp.where(i > 0, glu(ap_ref[...], gp_ref[...]), zero_halo)
    hbuf[HALO:HALO + tm] = glu(a_ref[...], g_ref[...])
    hbuf[HALO + tm:] = jnp.where(i < last, glu(an_ref[...], gn_ref[...]), zero_halo)

    rc = 64
    span = rc + 2 * HALO
    for r0 in range(0, tm, rc):
        for c in range(CONV_DIM // LANES):
            cs = slice(c * LANES, (c + 1) * LANES)
            col = hbuf[r0:r0 + span, cs]
            acc = jnp.zeros((rc, LANES), F32)
            for res in range(SUBLANES):
                rolled = col if res == 0 else pltpu.roll(col, span - res, 0)
                for q in range(2 * HALO // SUBLANES):
                    k = q * SUBLANES + res - (HALO - CONV_PAD)
                    if 0 <= k < CONV_K:
                        acc = acc + rolled[q * SUBLANES:q * SUBLANES + rc] * cw_ref[k:k + 1, cs]
            cbuf[r0:r0 + rc, cs] = acc

    cv = cbuf[...] + cb_ref[...]
    mu = jnp.sum(cv, axis=-1, keepdims=True) * (1.0 / CONV_DIM)
    d = cv - mu
    var = jnp.sum(d * d, axis=-1, keepdims=True) * (1.0 / CONV_DIM)
    ln = d * lax.rsqrt(var + EPS) * lg_ref[...] + lb_ref[...]
    act = (ln * _sigmoid(ln)).astype(BF16)

    conv_o = jnp.dot(act, wco_ref[...], preferred_element_type=F32)
    mix = (_sigmoid(ga_ref[...].astype(F32)) * attn_o
           + _sigmoid(gc_ref[...].astype(F32)) * conv_o).astype(BF16)
    y = jnp.dot(mix, wout_ref[...], preferred_element_type=F32)
    x1 = x_ref[...] + gt1_ref[...] * y
    x1_ref[...] = x1
    ms = jnp.sum(x1 * x1, axis=-1, keepdims=True) * (1.0 / D_MODEL)
    n = x1 * lax.rsqrt(ms + EPS) * gf_ref[...]
    h2_ref[...] = (n * (1.0 + sc2_ref[...]) + sh2_ref[...]).astype(BF16)


def _mixer_call(x, ugg, attn, mod, conv_w, conv_b, ln_g, ln_b, g_ffn, wao, wco, wout, tm):
    B, S, _ = x.shape
    hb = tm // HALO
    nh = S // HALO
    row = lambda w, col: pl.BlockSpec((None, tm, w), lambda b, i: (b, i, col))
    prev = lambda col: pl.BlockSpec((None, HALO, CONV_DIM),
                                    lambda b, i: (b, jnp.maximum(i * hb - 1, 0), col))
    nxt = lambda col: pl.BlockSpec((None, HALO, CONV_DIM),
                                   lambda b, i: (b, jnp.minimum((i + 1) * hb, nh - 1), col))
    vec = lambda k: pl.BlockSpec((None, None, 1, D_MODEL), lambda b, i: (b, k, 0, 0))
    const = lambda shape: pl.BlockSpec(shape, lambda b, i: (0, 0), pipeline_mode=pl.Buffered(1))
    return pl.pallas_call(
        functools.partial(_mixer_kernel, tm=tm),
        grid=(B, S // tm),
        in_specs=[row(D_MODEL, 0), row(CONV_DIM, 0), row(CONV_DIM, 1),
                  prev(0), prev(1), nxt(0), nxt(1),
                  row(D_MODEL, 1), row(D_MODEL, 2), row(Q_DIM, 0),
                  vec(2), vec(4), vec(3),
                  const((CONV_K + 1, CONV_DIM)), const((1, CONV_DIM)), const((1, CONV_DIM)),
                  const((1, CONV_DIM)), const((1, D_MODEL)),
                  const((Q_DIM, D_MODEL)), const((CONV_DIM, D_MODEL)), const((D_MODEL, D_MODEL))],
        out_specs=[row(D_MODEL, 0), row(D_MODEL, 0)],
        out_shape=[jax.ShapeDtypeStruct((B, S, D_MODEL), F32),
                   jax.ShapeDtypeStruct((B, S, D_MODEL), BF16)],
        scratch_shapes=[pltpu.VMEM((tm + 2 * HALO, CONV_DIM), F32),
                        pltpu.VMEM((tm, CONV_DIM), F32)],
        compiler_params=_params(("arbitrary", "arbitrary")),
        name="mixer_tail",
    )(x, ugg, ugg, ugg, ugg, ugg, ugg, ugg, ugg, attn, mod, mod, mod,
      conv_w, conv_b, ln_g, ln_b, g_ffn, wao, wco, wout)


FFN_X1_CHUNKS = 8


def _ffn_kernel(h_ref, x1_ref, gt2_ref, gfin_ref, wa_ref, wb_ref, wo_ref, o_ref, *, final_norm):
    f = pl.program_id(2)
    chunk_rows = x1_ref.shape[0]

    def ffn_tile_sum(first):
        h = h_ref[...]
        a = jnp.dot(h, wa_ref[...], preferred_element_type=F32)
        b = jnp.dot(h, wb_ref[...], preferred_element_type=F32)
        g = (a * _sigmoid(a) * b).astype(BF16)
        for n in range(D_MODEL // MXU_N):
            sl = slice(n * MXU_N, (n + 1) * MXU_N)
            y = gt2_ref[:, sl] * jnp.dot(g, wo_ref[:, sl], preferred_element_type=F32)
            if first:
                o_ref[:, sl] = y
            else:
                o_ref[:, sl] += y

    @pl.when(f == 0)
    def _():
        ffn_tile_sum(True)

    @pl.when(f > 0)
    def _():
        ffn_tile_sum(False)
        r0 = pl.multiple_of((jnp.minimum(f, FFN_X1_CHUNKS) - 1) * chunk_rows, chunk_rows)
        rows = pl.ds(r0, chunk_rows)
        o_ref[rows, :] += jnp.where(f <= FFN_X1_CHUNKS, x1_ref[...], 0.0)

    if final_norm:
        @pl.when(f == pl.num_programs(2) - 1)
        def _():
            x = o_ref[...]
            ms = jnp.sum(x * x, axis=-1, keepdims=True) * (1.0 / D_MODEL)
            o_ref[...] = x * lax.rsqrt(ms + EPS) * gfin_ref[...]


def _ffn_call(h2, x1, mod, g_final, w_ffn_in, w_ffn_out, tm, tf, final_norm):
    B, S, _ = x1.shape
    nf = D_FF // tf
    row = lambda: pl.BlockSpec((None, tm, D_MODEL), lambda b, i, f: (b, i, 0))
    assert nf > FFN_X1_CHUNKS and tm % (8 * FFN_X1_CHUNKS) == 0
    x1_spec = pl.BlockSpec(
        (None, tm // FFN_X1_CHUNKS, D_MODEL),
        lambda b, i, f: (b, i * FFN_X1_CHUNKS + jnp.clip(f - 1, 0, FFN_X1_CHUNKS - 1), 0))
    return pl.pallas_call(
        functools.partial(_ffn_kernel, final_norm=final_norm),
        grid=(B, S // tm, nf),
        in_specs=[row(), x1_spec,
                  pl.BlockSpec((None, None, 1, D_MODEL), lambda b, i, f: (b, 5, 0, 0)),
                  pl.BlockSpec((1, D_MODEL), lambda b, i, f: (0, 0)),
                  pl.BlockSpec((D_MODEL, tf), lambda b, i, f: (0, f)),
                  pl.BlockSpec((D_MODEL, tf), lambda b, i, f: (0, f + nf)),
                  pl.BlockSpec((tf, D_MODEL), lambda b, i, f: (f, 0))],
        out_specs=row(),
        out_shape=jax.ShapeDtypeStruct((B, S, D_MODEL), F32),
        compiler_params=_params(("arbitrary", "arbitrary", "arbitrary")),
        name="swiglu_ffn",
    )(h2, x1, mod, g_final, w_ffn_in, w_ffn_in, w_ffn_out)


def _rope_tables(seq):
    inv = 1.0 / (ROPE_THETA ** (jnp.arange(0, HEAD_DIM, 2, dtype=F32) / HEAD_DIM))
    ang = jnp.arange(seq, dtype=F32)[:, None] * inv[None, :]
    cos, sin = jnp.cos(ang), jnp.sin(ang)
    reps = LANES // HEAD_DIM
    cos_t = jnp.tile(jnp.concatenate([cos, cos], axis=-1), (1, reps))
    sin_t = jnp.tile(jnp.concatenate([-sin, sin], axis=-1), (1, reps))
    return cos_t, sin_t


def _encoder(x, mod, layers, g_final):
    B, S, _ = x.shape
    cos_t, sin_t = _rope_tables(S)
    for l, lw in enumerate(layers):
        m = mod[l]
        ugg, qt, kv = _inproj_call(x, m, lw["g_mix"], cos_t, sin_t, lw["w_in"], tm=1024)
        attn = _attn_call(qt, kv, lw["sink"], tq=512)
        x1, h2 = _mixer_call(x, ugg, attn, m, lw["conv_w"], lw["conv_b"], lw["ln_g"], lw["ln_b"],
                             lw["g_ffn"], lw["wao"], lw["wco"], lw["wout"], tm=256)
        x = _ffn_call(h2, x1, m, g_final, lw["w_ffn_in"], lw["w_ffn_out"], tm=1024, tf=512,
                      final_norm=(l == len(layers) - 1))
    return x


def kernel(x_prompt, x_sample, c_prompt, c_sample, w_ada, b_ada, g_mix, w_in, attn_sink, w_attn_o,
           conv_w, conv_b, conv_ln_g, conv_ln_b, w_conv_o, w_out, g_ffn, w_ffn_in, w_ffn_out, g_final):
    depth = w_in.shape[0]
    nb_p, nb_s = c_prompt.shape[0], c_sample.shape[0]
    rows = -(-(nb_p + nb_s) // 8) * 8
    c_all = jnp.concatenate([c_prompt, c_sample], axis=0)
    c_pad = jnp.pad(c_all, ((0, rows - nb_p - nb_s), (0, 0)))

    layers, mods_p, mods_s = [], [], []
    for l in range(depth):
        mod = _mod_call(c_pad, w_ada[l], b_ada[l]).reshape(rows, N_MOD, 1, D_MODEL)
        mods_p.append(mod[:nb_p])
        mods_s.append(mod[nb_p:nb_p + nb_s])
        layers.append(dict(
            g_mix=g_mix[l].reshape(1, D_MODEL), w_in=w_in[l].astype(BF16), sink=attn_sink[l],
            wao=w_attn_o[l].astype(BF16),
            conv_w=jnp.pad(conv_w[l], ((0, 1), (0, 0))), conv_b=conv_b[l].reshape(1, CONV_DIM),
            ln_g=conv_ln_g[l].reshape(1, CONV_DIM), ln_b=conv_ln_b[l].reshape(1, CONV_DIM),
            wco=w_conv_o[l].astype(BF16), wout=w_out[l].astype(BF16),
            g_ffn=g_ffn[l].reshape(1, D_MODEL), w_ffn_in=w_ffn_in[l].astype(BF16),
            w_ffn_out=w_ffn_out[l].astype(BF16)))
    gfin = g_final.reshape(1, D_MODEL)
    y_prompt = _encoder(x_prompt, mods_p, layers, gfin)
    y_sample = _encoder(x_sample, mods_s, layers, gfin)
    return (y_prompt, y_sample)
```

```python
import functools
import math

import jax
import jax.numpy as jnp
from jax import lax
from jax.experimental import pallas as pl
from jax.experimental.pallas import tpu as pltpu

D_MODEL = 2048
N_HEADS = 16
N_KV_HEADS = 4
GROUP = N_HEADS // N_KV_HEADS
HEAD_DIM = 64
Q_DIM = N_HEADS * HEAD_DIM
KV_DIM = N_KV_HEADS * HEAD_DIM
BLOCK = 128
WINDOW = 128
ROPE_THETA = 10000.0
CONV_DIM = 1024
CONV_K = 31
CONV_PAD = CONV_K // 2
D_FF = int(math.ceil(8 * D_MODEL / 3 / 256) * 256)
N_MOD = 6
EPS = 1e-6
IN_DIM = Q_DIM + 2 * KV_DIM + 2 * CONV_DIM + 2 * D_MODEL

LANES = 128
SUBLANES = 8
MXU_N = 256
HALO = 16
QKV_TILE = Q_DIM + 2 * KV_DIM
Q_ROWS = GROUP * HEAD_DIM
LOG2E = 1.4426950408889634
Q_SCALE = HEAD_DIM ** -0.5 * LOG2E

VMEM_LIMIT = 56 * 1024 * 1024

F32 = jnp.float32
BF16 = jnp.bfloat16


def _sigmoid(x):
    return 0.5 * jnp.tanh(0.5 * x) + 0.5


def _params(sem):
    return pltpu.CompilerParams(dimension_semantics=sem, vmem_limit_bytes=VMEM_LIMIT)


def _mod_kernel(c_ref, w_ref, b_ref, o_ref):
    c = c_ref[...]
    sc = c * _sigmoid(c)
    o_ref[...] = jnp.dot(sc, w_ref[...], precision=lax.Precision.HIGHEST,
                         preferred_element_type=F32) + b_ref[...]


def _mod_call(c_pad, w_ada, b_ada):
    rows = c_pad.shape[0]
    n = w_ada.shape[1]
    tn = 1024
    return pl.pallas_call(
        _mod_kernel,
        grid=(n // tn,),
        in_specs=[pl.BlockSpec((rows, D_MODEL), lambda j: (0, 0)),
                  pl.BlockSpec((D_MODEL, tn), lambda j: (0, j)),
                  pl.BlockSpec((1, tn), lambda j: (0, j))],
        out_specs=pl.BlockSpec((rows, tn), lambda j: (0, j)),
        out_shape=jax.ShapeDtypeStruct((rows, n), F32),
        compiler_params=_params(("arbitrary",)),
        name="adaln_mod",
    )(c_pad, w_ada, b_ada.reshape(1, n))


def _inproj_kernel(x_ref, sc_ref, sh_ref, g_ref, cos_ref, sin_ref, w_ref, ugg_ref, qt_ref, kv_ref,
                   h_ref):
    j = pl.program_id(2)

    @pl.when(j == 0)
    def _():
        x = x_ref[...]
        ms = jnp.sum(x * x, axis=-1, keepdims=True) * (1.0 / D_MODEL)
        y = x * lax.rsqrt(ms + EPS) * g_ref[...]
        h_ref[...] = (y * (1.0 + sc_ref[...]) + sh_ref[...]).astype(BF16)

        lane = lax.broadcasted_iota(jnp.int32, (1, LANES), 1)
        first_half = (lane % HEAD_DIM) < (HEAD_DIM // 2)
        cos = cos_ref[...]
        sin = sin_ref[...]

        def rope(xc):
            partner = jnp.where(first_half, pltpu.roll(xc, LANES - HEAD_DIM // 2, 1),
                                pltpu.roll(xc, HEAD_DIM // 2, 1))
            return xc * cos + partner * sin

        h = h_ref[...]
        for c in range(QKV_TILE // MXU_N):
            acc = jnp.dot(h, w_ref[:, c * MXU_N:(c + 1) * MXU_N], preferred_element_type=F32)
            for half in range(MXU_N // LANES):
                col = c * MXU_N + half * LANES
                xc = acc[:, half * LANES:(half + 1) * LANES]
                if col < Q_DIM:
                    yq = rope(xc) * Q_SCALE
                    qt_ref[col:col + LANES, :] = yq.T.astype(BF16)
                elif col < Q_DIM + KV_DIM:
                    kv_ref[:, col - Q_DIM:col - Q_DIM + LANES] = rope(xc).astype(BF16)
                else:
                    kv_ref[:, col - Q_DIM:col - Q_DIM + LANES] = xc.astype(BF16)

    @pl.when(j > 0)
    def _():
        ugg_ref[...] = jnp.dot(h_ref[...], w_ref[...], preferred_element_type=F32).astype(BF16)


def _inproj_call(x, mod, g_mix, cos_t, sin_t, w_in, tm):
    B, S, _ = x.shape
    tn = QKV_TILE
    n_tiles = IN_DIM // tn
    vec = lambda k: pl.BlockSpec((None, None, 1, D_MODEL), lambda b, i, j: (b, k, 0, 0))
    return pl.pallas_call(
        _inproj_kernel,
        grid=(B, S // tm, n_tiles),
        in_specs=[pl.BlockSpec((None, tm, D_MODEL), lambda b, i, j: (b, i, 0)),
                  vec(1), vec(0),
                  pl.BlockSpec((1, D_MODEL), lambda b, i, j: (0, 0)),
                  pl.BlockSpec((tm, LANES), lambda b, i, j: (i, 0)),
                  pl.BlockSpec((tm, LANES), lambda b, i, j: (i, 0)),
                  pl.BlockSpec((D_MODEL, tn), lambda b, i, j: (0, j))],
        out_specs=[pl.BlockSpec((None, tm, tn), lambda b, i, j: (b, i, jnp.maximum(j - 1, 0))),
                   pl.BlockSpec((None, Q_DIM, tm), lambda b, i, j: (b, 0, i)),
                   pl.BlockSpec((None, tm, 2 * KV_DIM), lambda b, i, j: (b, i, 0))],
        out_shape=[jax.ShapeDtypeStruct((B, S, IN_DIM - tn), BF16),
                   jax.ShapeDtypeStruct((B, Q_DIM, S), BF16),
                   jax.ShapeDtypeStruct((B, S, 2 * KV_DIM), BF16)],
        scratch_shapes=[pltpu.VMEM((tm, D_MODEL), BF16)],
        compiler_params=_params(("arbitrary", "arbitrary", "arbitrary")),
        name="inproj",
    )(x, mod, mod, g_mix, cos_t, sin_t, w_in)


ATTN_WAVE = 16


def _attn_kernel(sink_ref, qt_ref, kvp_ref, kvm_ref, kvn_ref, o_ref, kvbuf, *, tq, seq):
    i = pl.program_id(1)
    nq = tq // BLOCK
    span = 3 * BLOCK
    cols = GROUP * BLOCK

    kvbuf[0:BLOCK] = kvp_ref[...]
    kvbuf[BLOCK:BLOCK + tq] = kvm_ref[...]
    kvbuf[BLOCK + tq:] = kvn_ref[...]

    kj = lax.broadcasted_iota(jnp.int32, (span, cols), 0)
    qi = lax.broadcasted_iota(jnp.int32, (span, cols), 1) % BLOCK
    neg = jnp.float32(-jnp.inf)
    in_window = jnp.where(jnp.abs(kj - BLOCK - qi) <= WINDOW, 0.0, neg)
    last_blk = seq // BLOCK - 1
    tn_dims = (((0,), (0,)), ((), ()))

    def scores_of(qb, kh, bias):
        r0 = qb * BLOCK
        qt = qt_ref[kh * Q_ROWS:(kh + 1) * Q_ROWS, r0:r0 + BLOCK]
        qt4 = jnp.concatenate([qt[g * HEAD_DIM:(g + 1) * HEAD_DIM] for g in range(GROUP)], axis=1)
        k_all = kvbuf[r0:r0 + span, kh * HEAD_DIM:(kh + 1) * HEAD_DIM]
        s = jnp.dot(k_all, qt4, preferred_element_type=F32)
        return jnp.concatenate([s[0:BLOCK] + bias[0], s[BLOCK:2 * BLOCK], s[2 * BLOCK:] + bias[1]],
                               axis=0)

    def softmax_of(kh, s):
        sink = jnp.concatenate(
            [jnp.full((1, BLOCK), sink_ref[kh * GROUP + g] * LOG2E, F32) for g in range(GROUP)],
            axis=1)
        m = jnp.maximum(jnp.max(s, axis=0, keepdims=True), sink)
        p = jnp.exp2(s - m)
        return p.astype(BF16), jnp.sum(p, axis=0, keepdims=True) + jnp.exp2(sink - m)

    def values_of(qb, kh, p, den):
        r0 = qb * BLOCK
        v_all = kvbuf[r0:r0 + span, KV_DIM + kh * HEAD_DIM:KV_DIM + (kh + 1) * HEAD_DIM]
        ot = lax.dot_general(v_all, p, tn_dims, preferred_element_type=F32)
        return ot * (1.0 / den)

    def store_out(qb, kh, ot):
        r0 = qb * BLOCK
        ot4 = jnp.concatenate([ot[:, g * BLOCK:(g + 1) * BLOCK] for g in range(GROUP)], axis=0)
        o_ref[r0:r0 + BLOCK, kh * Q_ROWS:(kh + 1) * Q_ROWS] = ot4.T.astype(BF16)

    chains = [(qb, kh) for qb in range(nq) for kh in range(N_KV_HEADS)]
    for w0 in range(0, len(chains), ATTN_WAVE):
        wave = chains[w0:w0 + ATTN_WAVE]
        biases = {}
        for qb, _ in wave:
            if qb not in biases:
                blk = i * nq + qb
                biases[qb] = (in_window[0:BLOCK] + jnp.where(blk > 0, 0.0, neg),
                              in_window[2 * BLOCK:] + jnp.where(blk < last_blk, 0.0, neg))
        scores = [scores_of(qb, kh, biases[qb]) for qb, kh in wave]
        soft = [softmax_of(kh, s) for (_, kh), s in zip(wave, scores)]
        outs = [values_of(qb, kh, p, den) for (qb, kh), (p, den) in zip(wave, soft)]
        for (qb, kh), ot in zip(wave, outs):
            store_out(qb, kh, ot)


def _attn_call(qt, kv, sink, tq):
    B, S, _ = kv.shape
    nq = tq // BLOCK
    nb = S // BLOCK
    return pl.pallas_call(
        functools.partial(_attn_kernel, tq=tq, seq=S),
        grid=(B, S // tq),
        in_specs=[pl.BlockSpec(memory_space=pltpu.SMEM),
                  pl.BlockSpec((None, Q_DIM, tq), lambda b, i: (b, 0, i)),
                  pl.BlockSpec((None, BLOCK, 2 * KV_DIM),
                               lambda b, i: (b, jnp.maximum(i * nq - 1, 0), 0)),
                  pl.BlockSpec((None, tq, 2 * KV_DIM), lambda b, i: (b, i, 0)),
                  pl.BlockSpec((None, BLOCK, 2 * KV_DIM),
                               lambda b, i: (b, jnp.minimum((i + 1) * nq, nb - 1), 0))],
        out_specs=pl.BlockSpec((None, tq, Q_DIM), lambda b, i: (b, i, 0)),
        out_shape=jax.ShapeDtypeStruct((B, S, Q_DIM), BF16),
        scratch_shapes=[pltpu.VMEM((tq + 2 * BLOCK, 2 * KV_DIM), BF16)],
        compiler_params=_params(("arbitrary", "arbitrary")),
        name="band_attn",
    )(sink, qt, kv, kv, kv)


def _mixer_kernel(x_ref, a_ref, g_ref, ap_ref, gp_ref, an_ref, gn_ref, ga_ref, gc_ref, attn_ref,
                  gt1_ref, sc2_ref, sh2_ref, cw_ref, cb_ref, lg_ref, lb_ref, gf_ref,
                  wao_ref, wco_ref, wout_ref, x1_ref, h2_ref, hbuf, cbuf, *, tm):
    i = pl.program_id(1)
    last = pl.num_programs(1) - 1

    attn_o = jnp.dot(attn_ref[...], wao_ref[...], preferred_element_type=F32)

    def glu(a, g):
        return a.astype(F32) * _sigmoid(g.astype(F32))

    zero_halo = jnp.zeros((HALO, CONV_DIM), F32)
    hbuf[0:HALO] = jnp.where(i > 0, glu(ap_ref[...], gp_ref[...]), zero_halo)
    hbuf[HALO:HALO + tm] = glu(a_ref[...], g_ref[...])
    hbuf[HALO + tm:] = jnp.where(i < last, glu(an_ref[...], gn_ref[...]), zero_halo)

    rc = 64
    span = rc + 2 * HALO
    for r0 in range(0, tm, rc):
        for c in range(CONV_DIM // LANES):
            cs = slice(c * LANES, (c + 1) * LANES)
            col = hbuf[r0:r0 + span, cs]
            acc = jnp.zeros((rc, LANES), F32)
            for res in range(SUBLANES):
                rolled = col if res == 0 else pltpu.roll(col, span - res, 0)
                for q in range(2 * HALO // SUBLANES):
                    k = q * SUBLANES + res - (HALO - CONV_PAD)
                    if 0 <= k < CONV_K:
                        acc = acc + rolled[q * SUBLANES:q * SUBLANES + rc] * cw_ref[k:k + 1, cs]
            cbuf[r0:r0 + rc, cs] = acc

    cv = cbuf[...] + cb_ref[...]
    mu = jnp.sum(cv, axis=-1, keepdims=True) * (1.0 / CONV_DIM)
    d = cv - mu
    var = jnp.sum(d * d, axis=-1, keepdims=True) * (1.0 / CONV_DIM)
    ln = d * lax.rsqrt(var + EPS) * lg_ref[...] + lb_ref[...]
    act = (ln * _sigmoid(ln)).astype(BF16)

    conv_o = jnp.dot(act, wco_ref[...], preferred_element_type=F32)
    mix = (_sigmoid(ga_ref[...].astype(F32)) * attn_o
           + _sigmoid(gc_ref[...].astype(F32)) * conv_o).astype(BF16)
    y = jnp.dot(mix, wout_ref[...], preferred_element_type=F32)
    x1 = x_ref[...] + gt1_ref[...] * y
    x1_ref[...] = x1
    ms = jnp.sum(x1 * x1, axis=-1, keepdims=True) * (1.0 / D_MODEL)
    n = x1 * lax.rsqrt(ms + EPS) * gf_ref[...]
    h2_ref[...] = (n * (1.0 + sc2_ref[...]) + sh2_ref[...]).astype(BF16)


def _mixer_call(x, ugg, attn, mod, conv_w, conv_b, ln_g, ln_b, g_ffn, wao, wco, wout, tm):
    B, S, _ = x.shape
    hb = tm // HALO
    nh = S // HALO
    row = lambda w, col: pl.BlockSpec((None, tm, w), lambda b, i: (b, i, col))
    prev = lambda col: pl.BlockSpec((None, HALO, CONV_DIM),
                                    lambda b, i: (b, jnp.maximum(i * hb - 1, 0), col))
    nxt = lambda col: pl.BlockSpec((None, HALO, CONV_DIM),
                                   lambda b, i: (b, jnp.minimum((i + 1) * hb, nh - 1), col))
    vec = lambda k: pl.BlockSpec((None, None, 1, D_MODEL), lambda b, i: (b, k, 0, 0))
    const = lambda shape: pl.BlockSpec(shape, lambda b, i: (0, 0), pipeline_mode=pl.Buffered(1))
    return pl.pallas_call(
        functools.partial(_mixer_kernel, tm=tm),
        grid=(B, S // tm),
        in_specs=[row(D_MODEL, 0), row(CONV_DIM, 0), row(CONV_DIM, 1),
                  prev(0), prev(1), nxt(0), nxt(1),
                  row(D_MODEL, 1), row(D_MODEL, 2), row(Q_DIM, 0),
                  vec(2), vec(4), vec(3),
                  const((CONV_K + 1, CONV_DIM)), const((1, CONV_DIM)), const((1, CONV_DIM)),
                  const((1, CONV_DIM)), const((1, D_MODEL)),
                  const((Q_DIM, D_MODEL)), const((CONV_DIM, D_MODEL)), const((D_MODEL, D_MODEL))],
        out_specs=[row(D_MODEL, 0), row(D_MODEL, 0)],
        out_shape=[jax.ShapeDtypeStruct((B, S, D_MODEL), F32),
                   jax.ShapeDtypeStruct((B, S, D_MODEL), BF16)],
        scratch_shapes=[pltpu.VMEM((tm + 2 * HALO, CONV_DIM), F32),
                        pltpu.VMEM((tm, CONV_DIM), F32)],
        compiler_params=_params(("arbitrary", "arbitrary")),
        name="mixer_tail",
    )(x, ugg, ugg, ugg, ugg, ugg, ugg, ugg, ugg, attn, mod, mod, mod,
      conv_w, conv_b, ln_g, ln_b, g_ffn, wao, wco, wout)


FFN_X1_CHUNKS = 8


def _ffn_kernel(h_ref, x1_ref, gt2_ref, gfin_ref, wa_ref, wb_ref, wo_ref, o_ref, *, final_norm):
    f = pl.program_id(2)
    chunk_rows = x1_ref.shape[0]

    def ffn_tile_sum(first):
        h = h_ref[...]
        a = jnp.dot(h, wa_ref[...], preferred_element_type=F32)
        b = jnp.dot(h, wb_ref[...], preferred_element_type=F32)
        g = (a * _sigmoid(a) * b).astype(BF16)
        for n in range(D_MODEL // MXU_N):
            sl = slice(n * MXU_N, (n + 1) * MXU_N)
            y = gt2_ref[:, sl] * jnp.dot(g, wo_ref[:, sl], preferred_element_type=F32)
            if first:
                o_ref[:, sl] = y
            else:
                o_ref[:, sl] += y

    @pl.when(f == 0)
    def _():
        ffn_tile_sum(True)

    @pl.when(f > 0)
    def _():
        ffn_tile_sum(False)
        r0 = pl.multiple_of((jnp.minimum(f, FFN_X1_CHUNKS) - 1) * chunk_rows, chunk_rows)
        rows = pl.ds(r0, chunk_rows)
        o_ref[rows, :] += jnp.where(f <= FFN_X1_CHUNKS, x1_ref[...], 0.0)

    if final_norm:
        @pl.when(f == pl.num_programs(2) - 1)
        def _():
            x = o_ref[...]
            ms = jnp.sum(x * x, axis=-1, keepdims=True) * (1.0 / D_MODEL)
            o_ref[...] = x * lax.rsqrt(ms + EPS) * gfin_ref[...]


def _ffn_call(h2, x1, mod, g_final, w_ffn_in, w_ffn_out, tm, tf, final_norm):
    B, S, _ = x1.shape
    nf = D_FF // tf
    row = lambda: pl.BlockSpec((None, tm, D_MODEL), lambda b, i, f: (b, i, 0))
    assert nf > FFN_X1_CHUNKS and tm % (8 * FFN_X1_CHUNKS) == 0
    x1_spec = pl.BlockSpec(
        (None, tm // FFN_X1_CHUNKS, D_MODEL),
        lambda b, i, f: (b, i * FFN_X1_CHUNKS + jnp.clip(f - 1, 0, FFN_X1_CHUNKS - 1), 0))
    return pl.pallas_call(
        functools.partial(_ffn_kernel, final_norm=final_norm),
        grid=(B, S // tm, nf),
        in_specs=[row(), x1_spec,
                  pl.BlockSpec((None, None, 1, D_MODEL), lambda b, i, f: (b, 5, 0, 0)),
                  pl.BlockSpec((1, D_MODEL), lambda b, i, f: (0, 0)),
                  pl.BlockSpec((D_MODEL, tf), lambda b, i, f: (0, f)),
                  pl.BlockSpec((D_MODEL, tf), lambda b, i, f: (0, f + nf)),
                  pl.BlockSpec((tf, D_MODEL), lambda b, i, f: (f, 0))],
        out_specs=row(),
        out_shape=jax.ShapeDtypeStruct((B, S, D_MODEL), F32),
        compiler_params=_params(("arbitrary", "arbitrary", "arbitrary")),
        name="swiglu_ffn",
    )(h2, x1, mod, g_final, w_ffn_in, w_ffn_in, w_ffn_out)


def _rope_tables(seq):
    inv = 1.0 / (ROPE_THETA ** (jnp.arange(0, HEAD_DIM, 2, dtype=F32) / HEAD_DIM))
    ang = jnp.arange(seq, dtype=F32)[:, None] * inv[None, :]
    cos, sin = jnp.cos(ang), jnp.sin(ang)
    reps = LANES // HEAD_DIM
    cos_t = jnp.tile(jnp.concatenate([cos, cos], axis=-1), (1, reps))
    sin_t = jnp.tile(jnp.concatenate([-sin, sin], axis=-1), (1, reps))
    return cos_t, sin_t


def _encoder(x, mod, layers, g_final):
    B, S, _ = x.shape
    cos_t, sin_t = _rope_tables(S)
    for l, lw in enumerate(layers):
        m = mod[l]
        ugg, qt, kv = _inproj_call(x, m, lw["g_mix"], cos_t, sin_t, lw["w_in"], tm=1024)
        attn = _attn_call(qt, kv, lw["sink"], tq=1024)
        x1, h2 = _mixer_call(x, ugg, attn, m, lw["conv_w"], lw["conv_b"], lw["ln_g"], lw["ln_b"],
                             lw["g_ffn"], lw["wao"], lw["wco"], lw["wout"], tm=256)
        x = _ffn_call(h2, x1, m, g_final, lw["w_ffn_in"], lw["w_ffn_out"], tm=1024, tf=512,
                      final_norm=(l == len(layers) - 1))
    return x


def kernel(x_prompt, x_sample, c_prompt, c_sample, w_ada, b_ada, g_mix, w_in, attn_sink, w_attn_o,
           conv_w, conv_b, conv_ln_g, conv_ln_b, w_conv_o, w_out, g_ffn, w_ffn_in, w_ffn_out, g_final):
    depth = w_in.shape[0]
    nb_p, nb_s = c_prompt.shape[0], c_sample.shape[0]
    rows = -(-(nb_p + nb_s) // 8) * 8
    c_all = jnp.concatenate([c_prompt, c_sample], axis=0)
    c_pad = jnp.pad(c_all, ((0, rows - nb_p - nb_s), (0, 0)))

    layers, mods_p, mods_s = [], [], []
    for l in range(depth):
        mod = _mod_call(c_pad, w_ada[l], b_ada[l]).reshape(rows, N_MOD, 1, D_MODEL)
        mods_p.append(mod[:nb_p])
        mods_s.append(mod[nb_p:nb_p + nb_s])
        layers.append(dict(
            g_mix=g_mix[l].reshape(1, D_MODEL), w_in=w_in[l].astype(BF16), sink=attn_sink[l],
            wao=w_attn_o[l].astype(BF16),
            conv_w=jnp.pad(conv_w[l], ((0, 1), (0, 0))), conv_b=conv_b[l].reshape(1, CONV_DIM),
            ln_g=conv_ln_g[l].reshape(1, CONV_DIM), ln_b=conv_ln_b[l].reshape(1, CONV_DIM),
            wco=w_conv_o[l].astype(BF16), wout=w_out[l].astype(BF16),
            g_ffn=g_ffn[l].reshape(1, D_MODEL), w_ffn_in=w_ffn_in[l].astype(BF16),
            w_ffn_out=w_ffn_out[l].astype(BF16)))
    gfin = g_final.reshape(1, D_MODEL)
    y_prompt = _encoder(x_prompt, mods_p, layers, gfin)
    y_sample = _encoder(x_sample, mods_s, layers, gfin)
    return (y_prompt, y_sample)
```

```python
import functools
import math

import jax
import jax.numpy as jnp
from jax import lax
from jax.experimental import pallas as pl
from jax.experimental.pallas import tpu as pltpu

D_MODEL = 2048
N_HEADS = 16
N_KV_HEADS = 4
GROUP = N_HEADS // N_KV_HEADS
HEAD_DIM = 64
Q_DIM = N_HEADS * HEAD_DIM
KV_DIM = N_KV_HEADS * HEAD_DIM
BLOCK = 128
WINDOW = 128
ROPE_THETA = 10000.0
CONV_DIM = 1024
CONV_K = 31
CONV_PAD = CONV_K // 2
D_FF = int(math.ceil(8 * D_MODEL / 3 / 256) * 256)
N_MOD = 6
EPS = 1e-6
IN_DIM = Q_DIM + 2 * KV_DIM + 2 * CONV_DIM + 2 * D_MODEL

LANES = 128
SUBLANES = 8
MXU_N = 256
HALO = 16
QKV_TILE = Q_DIM + 2 * KV_DIM
Q_ROWS = GROUP * HEAD_DIM
LOG2E = 1.4426950408889634
Q_SCALE = HEAD_DIM ** -0.5 * LOG2E

VMEM_LIMIT = 56 * 1024 * 1024

F32 = jnp.float32
BF16 = jnp.bfloat16


def _sigmoid(x):
    return 0.5 * jnp.tanh(0.5 * x) + 0.5


def _params(sem):
    return pltpu.CompilerParams(dimension_semantics=sem, vmem_limit_bytes=VMEM_LIMIT)


def _mod_kernel(c_ref, w_ref, b_ref, o_ref):
    c = c_ref[...]
    sc = c * _sigmoid(c)
    o_ref[...] = jnp.dot(sc, w_ref[...], precision=lax.Precision.HIGHEST,
                         preferred_element_type=F32) + b_ref[...]


def _mod_call(c_pad, w_ada, b_ada):
    rows = c_pad.shape[0]
    n = w_ada.shape[1]
    tn = 1024
    return pl.pallas_call(
        _mod_kernel,
        grid=(n // tn,),
        in_specs=[pl.BlockSpec((rows, D_MODEL), lambda j: (0, 0)),
                  pl.BlockSpec((D_MODEL, tn), lambda j: (0, j)),
                  pl.BlockSpec((1, tn), lambda j: (0, j))],
        out_specs=pl.BlockSpec((rows, tn), lambda j: (0, j)),
        out_shape=jax.ShapeDtypeStruct((rows, n), F32),
        compiler_params=_params(("arbitrary",)),
        name="adaln_mod",
    )(c_pad, w_ada, b_ada.reshape(1, n))


def _inproj_kernel(x_ref, sc_ref, sh_ref, g_ref, cos_ref, sin_ref, w_ref, ugg_ref, qt_ref, kv_ref,
                   h_ref):
    j = pl.program_id(2)

    @pl.when(j == 0)
    def _():
        x = x_ref[...]
        ms = jnp.sum(x * x, axis=-1, keepdims=True) * (1.0 / D_MODEL)
        y = x * lax.rsqrt(ms + EPS) * g_ref[...]
        h_ref[...] = (y * (1.0 + sc_ref[...]) + sh_ref[...]).astype(BF16)

        lane = lax.broadcasted_iota(jnp.int32, (1, LANES), 1)
        first_half = (lane % HEAD_DIM) < (HEAD_DIM // 2)
        cos = cos_ref[...]
        sin = sin_ref[...]

        def rope(xc):
            partner = jnp.where(first_half, pltpu.roll(xc, LANES - HEAD_DIM // 2, 1),
                                pltpu.roll(xc, HEAD_DIM // 2, 1))
            return xc * cos + partner * sin

        h = h_ref[...]
        for c in range(QKV_TILE // MXU_N):
            acc = jnp.dot(h, w_ref[:, c * MXU_N:(c + 1) * MXU_N], preferred_element_type=F32)
            for half in range(MXU_N // LANES):
                col = c * MXU_N + half * LANES
                xc = acc[:, half * LANES:(half + 1) * LANES]
                if col < Q_DIM:
                    yq = rope(xc) * Q_SCALE
                    qt_ref[col:col + LANES, :] = yq.T.astype(BF16)
                elif col < Q_DIM + KV_DIM:
                    kv_ref[:, col - Q_DIM:col - Q_DIM + LANES] = rope(xc).astype(BF16)
                else:
                    kv_ref[:, col - Q_DIM:col - Q_DIM + LANES] = xc.astype(BF16)

    @pl.when(j > 0)
    def _():
        ugg_ref[...] = jnp.dot(h_ref[...], w_ref[...], preferred_element_type=F32).astype(BF16)


def _inproj_call(x, mod, g_mix, cos_t, sin_t, w_in, tm):
    B, S, _ = x.shape
    tn = QKV_TILE
    n_tiles = IN_DIM // tn
    vec = lambda k: pl.BlockSpec((None, None, 1, D_MODEL), lambda b, i, j: (b, k, 0, 0))
    return pl.pallas_call(
        _inproj_kernel,
        grid=(B, S // tm, n_tiles),
        in_specs=[pl.BlockSpec((None, tm, D_MODEL), lambda b, i, j: (b, i, 0)),
                  vec(1), vec(0),
                  pl.BlockSpec((1, D_MODEL), lambda b, i, j: (0, 0)),
                  pl.BlockSpec((tm, LANES), lambda b, i, j: (i, 0)),
                  pl.BlockSpec((tm, LANES), lambda b, i, j: (i, 0)),
                  pl.BlockSpec((D_MODEL, tn), lambda b, i, j: (0, j))],
        out_specs=[pl.BlockSpec((None, tm, tn), lambda b, i, j: (b, i, jnp.maximum(j - 1, 0))),
                   pl.BlockSpec((None, Q_DIM, tm), lambda b, i, j: (b, 0, i)),
                   pl.BlockSpec((None, tm, 2 * KV_DIM), lambda b, i, j: (b, i, 0))],
        out_shape=[jax.ShapeDtypeStruct((B, S, IN_DIM - tn), BF16),
                   jax.ShapeDtypeStruct((B, Q_DIM, S), BF16),
                   jax.ShapeDtypeStruct((B, S, 2 * KV_DIM), BF16)],
        scratch_shapes=[pltpu.VMEM((tm, D_MODEL), BF16)],
        compiler_params=_params(("arbitrary", "arbitrary", "arbitrary")),
        name="inproj",
    )(x, mod, mod, g_mix, cos_t, sin_t, w_in)


ATTN_WAVE = 32


def _attn_kernel(sink_ref, qt_ref, kvp_ref, kvm_ref, kvn_ref, o_ref, kbuf, vbuf, *, tq, seq):
    i = pl.program_id(1)
    nq = tq // BLOCK
    span = 3 * BLOCK
    cols = GROUP * BLOCK

    for dst, src in ((slice(0, BLOCK), kvp_ref), (slice(BLOCK, BLOCK + tq), kvm_ref),
                     (slice(BLOCK + tq, 2 * BLOCK + tq), kvn_ref)):
        kbuf[dst, :] = src[:, 0:KV_DIM]
        ones = jnp.ones((dst.stop - dst.start, LANES - HEAD_DIM), BF16)
        for kh in range(N_KV_HEADS):
            v_kh = src[:, KV_DIM + kh * HEAD_DIM:KV_DIM + (kh + 1) * HEAD_DIM]
            vbuf[dst, kh * LANES:(kh + 1) * LANES] = jnp.concatenate([v_kh, ones], axis=1)

    kj = lax.broadcasted_iota(jnp.int32, (span, cols), 0)
    qi = lax.broadcasted_iota(jnp.int32, (span, cols), 1) % BLOCK
    neg = jnp.float32(-jnp.inf)
    in_window = jnp.where(jnp.abs(kj - BLOCK - qi) <= WINDOW, 0.0, neg)
    last_blk = seq // BLOCK - 1
    tn_dims = (((0,), (0,)), ((), ()))

    def scores_of(qb, kh, bias):
        r0 = qb * BLOCK
        qt = qt_ref[kh * Q_ROWS:(kh + 1) * Q_ROWS, r0:r0 + BLOCK]
        qt4 = jnp.concatenate([qt[g * HEAD_DIM:(g + 1) * HEAD_DIM] for g in range(GROUP)], axis=1)
        k_all = kbuf[r0:r0 + span, kh * HEAD_DIM:(kh + 1) * HEAD_DIM]
        s = jnp.dot(k_all, qt4, preferred_element_type=F32)
        return jnp.concatenate([s[0:BLOCK] + bias[0], s[BLOCK:2 * BLOCK], s[2 * BLOCK:] + bias[1]],
                               axis=0)

    def softmax_of(kh, s):
        sink = jnp.concatenate(
            [jnp.full((1, BLOCK), sink_ref[kh * GROUP + g] * LOG2E, F32) for g in range(GROUP)],
            axis=1)
        m = jnp.maximum(jnp.max(s, axis=0, keepdims=True), sink)
        return jnp.exp2(s - m).astype(BF16), jnp.exp2(sink - m)

    def values_of(qb, kh, p, sink_p):
        r0 = qb * BLOCK
        v_ext = vbuf[r0:r0 + span, kh * LANES:(kh + 1) * LANES]
        ot = lax.dot_general(v_ext, p, tn_dims, preferred_element_type=F32)
        den = ot[HEAD_DIM:HEAD_DIM + 1] + sink_p
        return ot[0:HEAD_DIM] * (1.0 / den)

    def store_out(qb, kh, ot):
        r0 = qb * BLOCK
        ot4 = jnp.concatenate([ot[:, g * BLOCK:(g + 1) * BLOCK] for g in range(GROUP)], axis=0)
        o_ref[r0:r0 + BLOCK, kh * Q_ROWS:(kh + 1) * Q_ROWS] = ot4.T.astype(BF16)

    chains = [(qb, kh) for qb in range(nq) for kh in range(N_KV_HEADS)]
    for w0 in range(0, len(chains), ATTN_WAVE):
        wave = chains[w0:w0 + ATTN_WAVE]
        biases = {}
        for qb, _ in wave:
            if qb not in biases:
                blk = i * nq + qb
                biases[qb] = (in_window[0:BLOCK] + jnp.where(blk > 0, 0.0, neg),
                              in_window[2 * BLOCK:] + jnp.where(blk < last_blk, 0.0, neg))
        scores = [scores_of(qb, kh, biases[qb]) for qb, kh in wave]
        soft = [softmax_of(kh, s) for (_, kh), s in zip(wave, scores)]
        outs = [values_of(qb, kh, p, den) for (qb, kh), (p, den) in zip(wave, soft)]
        for (qb, kh), ot in zip(wave, outs):
            store_out(qb, kh, ot)


def _attn_call(qt, kv, sink, tq):
    B, S, _ = kv.shape
    nq = tq // BLOCK
    nb = S // BLOCK
    return pl.pallas_call(
        functools.partial(_attn_kernel, tq=tq, seq=S),
        grid=(B, S // tq),
        in_specs=[pl.BlockSpec(memory_space=pltpu.SMEM),
                  pl.BlockSpec((None, Q_DIM, tq), lambda b, i: (b, 0, i)),
                  pl.BlockSpec((None, BLOCK, 2 * KV_DIM),
                               lambda b, i: (b, jnp.maximum(i * nq - 1, 0), 0)),
                  pl.BlockSpec((None, tq, 2 * KV_DIM), lambda b, i: (b, i, 0)),
                  pl.BlockSpec((None, BLOCK, 2 * KV_DIM),
                               lambda b, i: (b, jnp.minimum((i + 1) * nq, nb - 1), 0))],
        out_specs=pl.BlockSpec((None, tq, Q_DIM), lambda b, i: (b, i, 0)),
        out_shape=jax.ShapeDtypeStruct((B, S, Q_DIM), BF16),
        scratch_shapes=[pltpu.VMEM((tq + 2 * BLOCK, KV_DIM), BF16),
                        pltpu.VMEM((tq + 2 * BLOCK, N_KV_HEADS * LANES), BF16)],
        compiler_params=_params(("arbitrary", "arbitrary")),
        name="band_attn",
    )(sink, qt, kv, kv, kv)


def _mixer_kernel(x_ref, a_ref, g_ref, ap_ref, gp_ref, an_ref, gn_ref, ga_ref, gc_ref, attn_ref,
                  gt1_ref, sc2_ref, sh2_ref, cw_ref, cb_ref, lg_ref, lb_ref, gf_ref,
                  wao_ref, wco_ref, wout_ref, x1_ref, h2_ref, hbuf, cbuf, *, tm):
    i = pl.program_id(1)
    last = pl.num_programs(1) - 1

    attn_o = jnp.dot(attn_ref[...], wao_ref[...], preferred_element_type=F32)

    def glu(a, g):
        return a.astype(F32) * _sigmoid(g.astype(F32))

    zero_halo = jnp.zeros((HALO, CONV_DIM), F32)
    hbuf[0:HALO] = jnp.where(i > 0, glu(ap_ref[...], gp_ref[...]), zero_halo)
    hbuf[HALO:HALO + tm] = glu(a_ref[...], g_ref[...])
    hbuf[HALO + tm:] = jnp.where(i < last, glu(an_ref[...], gn_ref[...]), zero_halo)

    rc = 64
    span = rc + 2 * HALO
    for r0 in range(0, tm, rc):
        for c in range(CONV_DIM // LANES):
            cs = slice(c * LANES, (c + 1) * LANES)
            col = hbuf[r0:r0 + span, cs]
            acc = jnp.zeros((rc, LANES), F32)
            for res in range(SUBLANES):
                rolled = col if res == 0 else pltpu.roll(col, span - res, 0)
                for q in range(2 * HALO // SUBLANES):
                    k = q * SUBLANES + res - (HALO - CONV_PAD)
                    if 0 <= k < CONV_K:
                        acc = acc + rolled[q * SUBLANES:q * SUBLANES + rc] * cw_ref[k:k + 1, cs]
            cbuf[r0:r0 + rc, cs] = acc

    cv = cbuf[...] + cb_ref[...]
    mu = jnp.sum(cv, axis=-1, keepdims=True) * (1.0 / CONV_DIM)
    d = cv - mu
    var = jnp.sum(d * d, axis=-1, keepdims=True) * (1.0 / CONV_DIM)
    ln = d * lax.rsqrt(var + EPS) * lg_ref[...] + lb_ref[...]
    act = (ln * _sigmoid(ln)).astype(BF16)

    conv_o = jnp.dot(act, wco_ref[...], preferred_element_type=F32)
    mix = (_sigmoid(ga_ref[...].astype(F32)) * attn_o
           + _sigmoid(gc_ref[...].astype(F32)) * conv_o).astype(BF16)
    y = jnp.dot(mix, wout_ref[...], preferred_element_type=F32)
    x1 = x_ref[...] + gt1_ref[...] * y
    x1_ref[...] = x1
    ms = jnp.sum(x1 * x1, axis=-1, keepdims=True) * (1.0 / D_MODEL)
    n = x1 * lax.rsqrt(ms + EPS) * gf_ref[...]
    h2_ref[...] = (n * (1.0 + sc2_ref[...]) + sh2_ref[...]).astype(BF16)


def _mixer_call(x, ugg, attn, mod, conv_w, conv_b, ln_g, ln_b, g_ffn, wao, wco, wout, tm):
    B, S, _ = x.shape
    hb = tm // HALO
    nh = S // HALO
    row = lambda w, col: pl.BlockSpec((None, tm, w), lambda b, i: (b, i, col))
    prev = lambda col: pl.BlockSpec((None, HALO, CONV_DIM),
                                    lambda b, i: (b, jnp.maximum(i * hb - 1, 0), col))
    nxt = lambda col: pl.BlockSpec((None, HALO, CONV_DIM),
                                   lambda b, i: (b, jnp.minimum((i + 1) * hb, nh - 1), col))
    vec = lambda k: pl.BlockSpec((None, None, 1, D_MODEL), lambda b, i: (b, k, 0, 0))
    const = lambda shape: pl.BlockSpec(shape, lambda b, i: (0, 0), pipeline_mode=pl.Buffered(1))
    return pl.pallas_call(
        functools.partial(_mixer_kernel, tm=tm),
        grid=(B, S // tm),
        in_specs=[row(D_MODEL, 0), row(CONV_DIM, 0), row(CONV_DIM, 1),
                  prev(0), prev(1), nxt(0), nxt(1),
                  row(D_MODEL, 1), row(D_MODEL, 2), row(Q_DIM, 0),
                  vec(2), vec(4), vec(3),
                  const((CONV_K + 1, CONV_DIM)), const((1, CONV_DIM)), const((1, CONV_DIM)),
                  const((1, CONV_DIM)), const((1, D_MODEL)),
                  const((Q_DIM, D_MODEL)), const((CONV_DIM, D_MODEL)), const((D_MODEL, D_MODEL))],
        out_specs=[row(D_MODEL, 0), row(D_MODEL, 0)],
        out_shape=[jax.ShapeDtypeStruct((B, S, D_MODEL), F32),
                   jax.ShapeDtypeStruct((B, S, D_MODEL), BF16)],
        scratch_shapes=[pltpu.VMEM((tm + 2 * HALO, CONV_DIM), F32),
                        pltpu.VMEM((tm, CONV_DIM), F32)],
        compiler_params=_params(("arbitrary", "arbitrary")),
        name="mixer_tail",
    )(x, ugg, ugg, ugg, ugg, ugg, ugg, ugg, ugg, attn, mod, mod, mod,
      conv_w, conv_b, ln_g, ln_b, g_ffn, wao, wco, wout)


FFN_X1_CHUNKS = 8


def _ffn_kernel(h_ref, x1_ref, gt2_ref, gfin_ref, wa_ref, wb_ref, wo_ref, o_ref, *, final_norm):
    f = pl.program_id(2)
    chunk_rows = x1_ref.shape[0]

    def ffn_tile_sum(first):
        h = h_ref[...]
        a = jnp.dot(h, wa_ref[...], preferred_element_type=F32)
        b = jnp.dot(h, wb_ref[...], preferred_element_type=F32)
        g = (a * _sigmoid(a) * b).astype(BF16)
        for n in range(D_MODEL // MXU_N):
            sl = slice(n * MXU_N, (n + 1) * MXU_N)
            y = gt2_ref[:, sl] * jnp.dot(g, wo_ref[:, sl], preferred_element_type=F32)
            if first:
                o_ref[:, sl] = y
            else:
                o_ref[:, sl] += y

    @pl.when(f == 0)
    def _():
        ffn_tile_sum(True)

    @pl.when(f > 0)
    def _():
        ffn_tile_sum(False)
        r0 = pl.multiple_of((jnp.minimum(f, FFN_X1_CHUNKS) - 1) * chunk_rows, chunk_rows)
        rows = pl.ds(r0, chunk_rows)
        o_ref[rows, :] += jnp.where(f <= FFN_X1_CHUNKS, x1_ref[...], 0.0)

    if final_norm:
        @pl.when(f == pl.num_programs(2) - 1)
        def _():
            x = o_ref[...]
            ms = jnp.sum(x * x, axis=-1, keepdims=True) * (1.0 / D_MODEL)
            o_ref[...] = x * lax.rsqrt(ms + EPS) * gfin_ref[...]


def _ffn_call(h2, x1, mod, g_final, w_ffn_in, w_ffn_out, tm, tf, final_norm):
    B, S, _ = x1.shape
    nf = D_FF // tf
    row = lambda: pl.BlockSpec((None, tm, D_MODEL), lambda b, i, f: (b, i, 0))
    assert nf > FFN_X1_CHUNKS and tm % (8 * FFN_X1_CHUNKS) == 0
    x1_spec = pl.BlockSpec(
        (None, tm // FFN_X1_CHUNKS, D_MODEL),
        lambda b, i, f: (b, i * FFN_X1_CHUNKS + jnp.clip(f - 1, 0, FFN_X1_CHUNKS - 1), 0))
    return pl.pallas_call(
        functools.partial(_ffn_kernel, final_norm=final_norm),
        grid=(B, S // tm, nf),
        in_specs=[row(), x1_spec,
                  pl.BlockSpec((None, None, 1, D_MODEL), lambda b, i, f: (b, 5, 0, 0)),
                  pl.BlockSpec((1, D_MODEL), lambda b, i, f: (0, 0)),
                  pl.BlockSpec((D_MODEL, tf), lambda b, i, f: (0, f)),
                  pl.BlockSpec((D_MODEL, tf), lambda b, i, f: (0, f + nf)),
                  pl.BlockSpec((tf, D_MODEL), lambda b, i, f: (f, 0))],
        out_specs=row(),
        out_shape=jax.ShapeDtypeStruct((B, S, D_MODEL), F32),
        compiler_params=_params(("arbitrary", "arbitrary", "arbitrary")),
        name="swiglu_ffn",
    )(h2, x1, mod, g_final, w_ffn_in, w_ffn_in, w_ffn_out)


def _rope_tables(seq):
    inv = 1.0 / (ROPE_THETA ** (jnp.arange(0, HEAD_DIM, 2, dtype=F32) / HEAD_DIM))
    ang = jnp.arange(seq, dtype=F32)[:, None] * inv[None, :]
    cos, sin = jnp.cos(ang), jnp.sin(ang)
    reps = LANES // HEAD_DIM
    cos_t = jnp.tile(jnp.concatenate([cos, cos], axis=-1), (1, reps))
    sin_t = jnp.tile(jnp.concatenate([-sin, sin], axis=-1), (1, reps))
    return cos_t, sin_t


def _encoder(x, mod, layers, g_final):
    B, S, _ = x.shape
    cos_t, sin_t = _rope_tables(S)
    for l, lw in enumerate(layers):
        m = mod[l]
        ugg, qt, kv = _inproj_call(x, m, lw["g_mix"], cos_t, sin_t, lw["w_in"], tm=1024)
        attn = _attn_call(qt, kv, lw["sink"], tq=2048)
        x1, h2 = _mixer_call(x, ugg, attn, m, lw["conv_w"], lw["conv_b"], lw["ln_g"], lw["ln_b"],
                             lw["g_ffn"], lw["wao"], lw["wco"], lw["wout"], tm=256)
        x = _ffn_call(h2, x1, m, g_final, lw["w_ffn_in"], lw["w_ffn_out"], tm=1024, tf=512,
                      final_norm=(l == len(layers) - 1))
    return x


def kernel(x_prompt, x_sample, c_prompt, c_sample, w_ada, b_ada, g_mix, w_in, attn_sink, w_attn_o,
           conv_w, conv_b, conv_ln_g, conv_ln_b, w_conv_o, w_out, g_ffn, w_ffn_in, w_ffn_out, g_final):
    depth = w_in.shape[0]
    nb_p, nb_s = c_prompt.shape[0], c_sample.shape[0]
    rows = -(-(nb_p + nb_s) // 8) * 8
    c_all = jnp.concatenate([c_prompt, c_sample], axis=0)
    c_pad = jnp.pad(c_all, ((0, rows - nb_p - nb_s), (0, 0)))

    layers, mods_p, mods_s = [], [], []
    for l in range(depth):
        mod = _mod_call(c_pad, w_ada[l], b_ada[l]).reshape(rows, N_MOD, 1, D_MODEL)
        mods_p.append(mod[:nb_p])
        mods_s.append(mod[nb_p:nb_p + nb_s])
        layers.append(dict(
            g_mix=g_mix[l].reshape(1, D_MODEL), w_in=w_in[l].astype(BF16), sink=attn_sink[l],
            wao=w_attn_o[l].astype(BF16),
            conv_w=jnp.pad(conv_w[l], ((0, 1), (0, 0))), conv_b=conv_b[l].reshape(1, CONV_DIM),
            ln_g=conv_ln_g[l].reshape(1, CONV_DIM), ln_b=conv_ln_b[l].reshape(1, CONV_DIM),
            wco=w_conv_o[l].astype(BF16), wout=w_out[l].astype(BF16),
            g_ffn=g_ffn[l].reshape(1, D_MODEL), w_ffn_in=w_ffn_in[l].astype(BF16),
            w_ffn_out=w_ffn_out[l].astype(BF16)))
    gfin = g_final.reshape(1, D_MODEL)
    y_prompt = _encoder(x_prompt, mods_p, layers, gfin)
    y_sample = _encoder(x_sample, mods_s, layers, gfin)
    return (y_prompt, y_sample)
```

```python
import functools
import math

import jax
import jax.numpy as jnp
from jax import lax
from jax.experimental import pallas as pl
from jax.experimental.pallas import tpu as pltpu

D_MODEL = 2048
N_HEADS = 16
N_KV_HEADS = 4
GROUP = N_HEADS // N_KV_HEADS
HEAD_DIM = 64
Q_DIM = N_HEADS * HEAD_DIM
KV_DIM = N_KV_HEADS * HEAD_DIM
BLOCK = 128
WINDOW = 128
ROPE_THETA = 10000.0
CONV_DIM = 1024
CONV_K = 31
CONV_PAD = CONV_K // 2
D_FF = int(math.ceil(8 * D_MODEL / 3 / 256) * 256)
N_MOD = 6
EPS = 1e-6
IN_DIM = Q_DIM + 2 * KV_DIM + 2 * CONV_DIM + 2 * D_MODEL

LANES = 128
SUBLANES = 8
MXU_N = 256
HALO = 16
QKV_TILE = Q_DIM + 2 * KV_DIM
Q_ROWS = GROUP * HEAD_DIM
LOG2E = 1.4426950408889634
Q_SCALE = HEAD_DIM ** -0.5 * LOG2E

VMEM_LIMIT = 56 * 1024 * 1024

F32 = jnp.float32
BF16 = jnp.bfloat16


def _sigmoid(x):
    return 0.5 * jnp.tanh(0.5 * x) + 0.5


def _params(sem):
    return pltpu.CompilerParams(dimension_semantics=sem, vmem_limit_bytes=VMEM_LIMIT)


def _mod_kernel(c_ref, w_ref, b_ref, o_ref):
    c = c_ref[...]
    sc = c * _sigmoid(c)
    o_ref[...] = jnp.dot(sc, w_ref[...], precision=lax.Precision.HIGHEST,
                         preferred_element_type=F32) + b_ref[...]


def _mod_call(c_pad, w_ada, b_ada):
    rows = c_pad.shape[0]
    n = w_ada.shape[1]
    tn = 1024
    return pl.pallas_call(
        _mod_kernel,
        grid=(n // tn,),
        in_specs=[pl.BlockSpec((rows, D_MODEL), lambda j: (0, 0)),
                  pl.BlockSpec((D_MODEL, tn), lambda j: (0, j)),
                  pl.BlockSpec((1, tn), lambda j: (0, j))],
        out_specs=pl.BlockSpec((rows, tn), lambda j: (0, j)),
        out_shape=jax.ShapeDtypeStruct((rows, n), F32),
        compiler_params=_params(("arbitrary",)),
        name="adaln_mod",
    )(c_pad, w_ada, b_ada.reshape(1, n))


def _inproj_kernel(x_ref, sc_ref, sh_ref, g_ref, cos_ref, sin_ref, w_ref, ugg_ref, qt_ref, kv_ref,
                   h_ref):
    j = pl.program_id(2)

    @pl.when(j == 0)
    def _():
        x = x_ref[...]
        ms = jnp.sum(x * x, axis=-1, keepdims=True) * (1.0 / D_MODEL)
        y = x * lax.rsqrt(ms + EPS) * g_ref[...]
        h_ref[...] = (y * (1.0 + sc_ref[...]) + sh_ref[...]).astype(BF16)

        lane = lax.broadcasted_iota(jnp.int32, (1, LANES), 1)
        first_half = (lane % HEAD_DIM) < (HEAD_DIM // 2)
        cos = cos_ref[...]
        sin = sin_ref[...]

        def rope(xc):
            partner = jnp.where(first_half, pltpu.roll(xc, LANES - HEAD_DIM // 2, 1),
                                pltpu.roll(xc, HEAD_DIM // 2, 1))
            return xc * cos + partner * sin

        h = h_ref[...]
        for c in range(QKV_TILE // MXU_N):
            acc = jnp.dot(h, w_ref[:, c * MXU_N:(c + 1) * MXU_N], preferred_element_type=F32)
            for half in range(MXU_N // LANES):
                col = c * MXU_N + half * LANES
                xc = acc[:, half * LANES:(half + 1) * LANES]
                if col < Q_DIM:
                    yq = rope(xc) * Q_SCALE
                    qt_ref[col:col + LANES, :] = yq.T.astype(BF16)
                elif col < Q_DIM + KV_DIM:
                    kv_ref[:, col - Q_DIM:col - Q_DIM + LANES] = rope(xc).astype(BF16)
                else:
                    kv_ref[:, col - Q_DIM:col - Q_DIM + LANES] = xc.astype(BF16)

    @pl.when(j > 0)
    def _():
        ugg_ref[...] = jnp.dot(h_ref[...], w_ref[...], preferred_element_type=F32).astype(BF16)


def _inproj_call(x, mod, g_mix, cos_t, sin_t, w_in, tm):
    B, S, _ = x.shape
    tn = QKV_TILE
    n_tiles = IN_DIM // tn
    vec = lambda k: pl.BlockSpec((None, None, 1, D_MODEL), lambda b, i, j: (b, k, 0, 0))
    return pl.pallas_call(
        _inproj_kernel,
        grid=(B, S // tm, n_tiles),
        in_specs=[pl.BlockSpec((None, tm, D_MODEL), lambda b, i, j: (b, i, 0)),
                  vec(1), vec(0),
                  pl.BlockSpec((1, D_MODEL), lambda b, i, j: (0, 0)),
                  pl.BlockSpec((tm, LANES), lambda b, i, j: (i, 0)),
                  pl.BlockSpec((tm, LANES), lambda b, i, j: (i, 0)),
                  pl.BlockSpec((D_MODEL, tn), lambda b, i, j: (0, j))],
        out_specs=[pl.BlockSpec((None, tm, tn), lambda b, i, j: (b, i, jnp.maximum(j - 1, 0))),
                   pl.BlockSpec((None, Q_DIM, tm), lambda b, i, j: (b, 0, i)),
                   pl.BlockSpec((None, tm, 2 * KV_DIM), lambda b, i, j: (b, i, 0))],
        out_shape=[jax.ShapeDtypeStruct((B, S, IN_DIM - tn), BF16),
                   jax.ShapeDtypeStruct((B, Q_DIM, S), BF16),
                   jax.ShapeDtypeStruct((B, S, 2 * KV_DIM), BF16)],
        scratch_shapes=[pltpu.VMEM((tm, D_MODEL), BF16)],
        compiler_params=_params(("arbitrary", "arbitrary", "arbitrary")),
        name="inproj",
    )(x, mod, mod, g_mix, cos_t, sin_t, w_in)


ATTN_WAVE = 32


def _attn_kernel(sink_ref, qt_ref, kvp_ref, kvm_ref, kvn_ref, o_ref, kbuf, vbuf, *, tq, seq):
    i = pl.program_id(1)
    nq = tq // BLOCK
    span = 3 * BLOCK
    cols = GROUP * BLOCK

    for dst, src in ((slice(0, BLOCK), kvp_ref), (slice(BLOCK, BLOCK + tq), kvm_ref),
                     (slice(BLOCK + tq, 2 * BLOCK + tq), kvn_ref)):
        kbuf[dst, :] = src[:, 0:KV_DIM]
        ones = jnp.ones((dst.stop - dst.start, LANES - HEAD_DIM), BF16)
        for kh in range(N_KV_HEADS):
            v_kh = src[:, KV_DIM + kh * HEAD_DIM:KV_DIM + (kh + 1) * HEAD_DIM]
            vbuf[dst, kh * LANES:(kh + 1) * LANES] = jnp.concatenate([v_kh, ones], axis=1)

    kj = lax.broadcasted_iota(jnp.int32, (span, cols), 0)
    qi = lax.broadcasted_iota(jnp.int32, (span, cols), 1) % BLOCK
    neg = jnp.float32(-jnp.inf)
    in_window = jnp.where(jnp.abs(kj - BLOCK - qi) <= WINDOW, 0.0, neg)
    last_blk = seq // BLOCK - 1
    tn_dims = (((0,), (0,)), ((), ()))

    def scores_of(qb, kh, bias):
        r0 = qb * BLOCK
        qt = qt_ref[kh * Q_ROWS:(kh + 1) * Q_ROWS, r0:r0 + BLOCK]
        qt4 = jnp.concatenate([qt[g * HEAD_DIM:(g + 1) * HEAD_DIM] for g in range(GROUP)], axis=1)
        k_all = kbuf[r0:r0 + span, kh * HEAD_DIM:(kh + 1) * HEAD_DIM]
        s = jnp.dot(k_all, qt4, preferred_element_type=F32)
        return jnp.concatenate([s[0:BLOCK] + bias[0], s[BLOCK:2 * BLOCK], s[2 * BLOCK:] + bias[1]],
                               axis=0)

    def softmax_of(kh, s):
        sink = jnp.concatenate(
            [jnp.full((1, BLOCK), sink_ref[kh * GROUP + g] * LOG2E, F32) for g in range(GROUP)],
            axis=1)
        m = jnp.maximum(jnp.max(s, axis=0, keepdims=True), sink)
        return jnp.exp2(s - m).astype(BF16), jnp.exp2(sink - m)

    def values_of(qb, kh, p, sink_p):
        r0 = qb * BLOCK
        v_ext = vbuf[r0:r0 + span, kh * LANES:(kh + 1) * LANES]
        ot = lax.dot_general(v_ext, p, tn_dims, preferred_element_type=F32)
        den = ot[HEAD_DIM:HEAD_DIM + 1] + sink_p
        return ot[0:HEAD_DIM] * (1.0 / den)

    def store_out(qb, kh, ot):
        r0 = qb * BLOCK
        ot4 = jnp.concatenate([ot[:, g * BLOCK:(g + 1) * BLOCK] for g in range(GROUP)], axis=0)
        o_ref[r0:r0 + BLOCK, kh * Q_ROWS:(kh + 1) * Q_ROWS] = ot4.T.astype(BF16)

    chains = [(qb, kh) for qb in range(nq) for kh in range(N_KV_HEADS)]
    for w0 in range(0, len(chains), ATTN_WAVE):
        wave = chains[w0:w0 + ATTN_WAVE]
        biases = {}
        for qb, _ in wave:
            if qb not in biases:
                blk = i * nq + qb
                biases[qb] = (in_window[0:BLOCK] + jnp.where(blk > 0, 0.0, neg),
                              in_window[2 * BLOCK:] + jnp.where(blk < last_blk, 0.0, neg))
        scores = [scores_of(qb, kh, biases[qb]) for qb, kh in wave]
        soft = [softmax_of(kh, s) for (_, kh), s in zip(wave, scores)]
        outs = [values_of(qb, kh, p, den) for (qb, kh), (p, den) in zip(wave, soft)]
        for (qb, kh), ot in zip(wave, outs):
            store_out(qb, kh, ot)


def _attn_call(qt, kv, sink, tq):
    B, S, _ = kv.shape
    nq = tq // BLOCK
    nb = S // BLOCK
    return pl.pallas_call(
        functools.partial(_attn_kernel, tq=tq, seq=S),
        grid=(B, S // tq),
        in_specs=[pl.BlockSpec(memory_space=pltpu.SMEM),
                  pl.BlockSpec((None, Q_DIM, tq), lambda b, i: (b, 0, i)),
                  pl.BlockSpec((None, BLOCK, 2 * KV_DIM),
                               lambda b, i: (b, jnp.maximum(i * nq - 1, 0), 0)),
                  pl.BlockSpec((None, tq, 2 * KV_DIM), lambda b, i: (b, i, 0)),
                  pl.BlockSpec((None, BLOCK, 2 * KV_DIM),
                               lambda b, i: (b, jnp.minimum((i + 1) * nq, nb - 1), 0))],
        out_specs=pl.BlockSpec((None, tq, Q_DIM), lambda b, i: (b, i, 0)),
        out_shape=jax.ShapeDtypeStruct((B, S, Q_DIM), BF16),
        scratch_shapes=[pltpu.VMEM((tq + 2 * BLOCK, KV_DIM), BF16),
                        pltpu.VMEM((tq + 2 * BLOCK, N_KV_HEADS * LANES), BF16)],
        compiler_params=_params(("arbitrary", "arbitrary")),
        name="band_attn",
    )(sink, qt, kv, kv, kv)


def _mixer_kernel(x_ref, a_ref, g_ref, ap_ref, gp_ref, an_ref, gn_ref, ga_ref, gc_ref, attn_ref,
                  gt1_ref, sc2_ref, sh2_ref, cw_ref, cb_ref, lg_ref, lb_ref, gf_ref,
                  wao_ref, wco_ref, wout_ref, x1_ref, h2_ref, hbuf, cbuf, *, tm):
    i = pl.program_id(1)
    last = pl.num_programs(1) - 1

    attn_o = jnp.dot(attn_ref[...], wao_ref[...], preferred_element_type=F32)

    def glu(a, g):
        return a.astype(F32) * _sigmoid(g.astype(F32))

    zero_halo = jnp.zeros((HALO, CONV_DIM), F32)
    hbuf[0:HALO] = jnp.where(i > 0, glu(ap_ref[...], gp_ref[...]), zero_halo)
    hbuf[HALO:HALO + tm] = glu(a_ref[...], g_ref[...])
    hbuf[HALO + tm:] = jnp.where(i < last, glu(an_ref[...], gn_ref[...]), zero_halo)

    rc = 64
    span = rc + 2 * HALO
    for r0 in range(0, tm, rc):
        for c in range(CONV_DIM // LANES):
            cs = slice(c * LANES, (c + 1) * LANES)
            col = hbuf[r0:r0 + span, cs]
            acc = jnp.zeros((rc, LANES), F32)
            for res in range(SUBLANES):
                rolled = col if res == 0 else pltpu.roll(col, span - res, 0)
                for q in range(2 * HALO // SUBLANES):
                    k = q * SUBLANES + res - (HALO - CONV_PAD)
                    if 0 <= k < CONV_K:
                        acc = acc + rolled[q * SUBLANES:q * SUBLANES + rc] * cw_ref[k:k + 1, cs]
            cbuf[r0:r0 + rc, cs] = acc

    cv = cbuf[...] + cb_ref[...]
    mu = jnp.sum(cv, axis=-1, keepdims=True) * (1.0 / CONV_DIM)
    d = cv - mu
    var = jnp.sum(d * d, axis=-1, keepdims=True) * (1.0 / CONV_DIM)
    ln = d * lax.rsqrt(var + EPS) * lg_ref[...] + lb_ref[...]
    act = (ln * _sigmoid(ln)).astype(BF16)

    conv_o = jnp.dot(act, wco_ref[...], preferred_element_type=F32)
    mix = (_sigmoid(ga_ref[...].astype(F32)) * attn_o
           + _sigmoid(gc_ref[...].astype(F32)) * conv_o).astype(BF16)
    y = jnp.dot(mix, wout_ref[...], preferred_element_type=F32)
    x1 = x_ref[...] + gt1_ref[...] * y
    x1_ref[...] = x1
    ms = jnp.sum(x1 * x1, axis=-1, keepdims=True) * (1.0 / D_MODEL)
    n = x1 * lax.rsqrt(ms + EPS) * gf_ref[...]
    h2_ref[...] = (n * (1.0 + sc2_ref[...]) + sh2_ref[...]).astype(BF16)


def _mixer_call(x, ugg, attn, mod, conv_w, conv_b, ln_g, ln_b, g_ffn, wao, wco, wout, tm):
    B, S, _ = x.shape
    hb = tm // HALO
    nh = S // HALO
    row = lambda w, col: pl.BlockSpec((None, tm, w), lambda b, i: (b, i, col))
    prev = lambda col: pl.BlockSpec((None, HALO, CONV_DIM),
                                    lambda b, i: (b, jnp.maximum(i * hb - 1, 0), col))
    nxt = lambda col: pl.BlockSpec((None, HALO, CONV_DIM),
                                   lambda b, i: (b, jnp.minimum((i + 1) * hb, nh - 1), col))
    vec = lambda k: pl.BlockSpec((None, None, 1, D_MODEL), lambda b, i: (b, k, 0, 0))
    const = lambda shape: pl.BlockSpec(shape, lambda b, i: (0, 0), pipeline_mode=pl.Buffered(1))
    return pl.pallas_call(
        functools.partial(_mixer_kernel, tm=tm),
        grid=(B, S // tm),
        in_specs=[row(D_MODEL, 0), row(CONV_DIM, 0), row(CONV_DIM, 1),
                  prev(0), prev(1), nxt(0), nxt(1),
                  row(D_MODEL, 1), row(D_MODEL, 2), row(Q_DIM, 0),
                  vec(2), vec(4), vec(3),
                  const((CONV_K + 1, CONV_DIM)), const((1, CONV_DIM)), const((1, CONV_DIM)),
                  const((1, CONV_DIM)), const((1, D_MODEL)),
                  const((Q_DIM, D_MODEL)), const((CONV_DIM, D_MODEL)), const((D_MODEL, D_MODEL))],
        out_specs=[row(D_MODEL, 0), row(D_MODEL, 0)],
        out_shape=[jax.ShapeDtypeStruct((B, S, D_MODEL), F32),
                   jax.ShapeDtypeStruct((B, S, D_MODEL), BF16)],
        scratch_shapes=[pltpu.VMEM((tm + 2 * HALO, CONV_DIM), F32),
                        pltpu.VMEM((tm, CONV_DIM), F32)],
        compiler_params=_params(("arbitrary", "arbitrary")),
        name="mixer_tail",
    )(x, ugg, ugg, ugg, ugg, ugg, ugg, ugg, ugg, attn, mod, mod, mod,
      conv_w, conv_b, ln_g, ln_b, g_ffn, wao, wco, wout)


FFN_X1_CHUNKS = 8


def _ffn_kernel(h_ref, x1_ref, gt2_ref, gfin_ref, wa_ref, wb_ref, wo_ref, o_ref, *, final_norm):
    f = pl.program_id(2)
    chunk_rows = x1_ref.shape[0]

    def ffn_tile_sum(first, normalize=False):
        h = h_ref[...]
        a = jnp.dot(h, wa_ref[...], preferred_element_type=F32)
        b = jnp.dot(h, wb_ref[...], preferred_element_type=F32)
        g = (a * _sigmoid(a) * b).astype(BF16)
        sq = jnp.zeros((h_ref.shape[0], 1), F32)
        for n in range(D_MODEL // MXU_N):
            sl = slice(n * MXU_N, (n + 1) * MXU_N)
            y = gt2_ref[:, sl] * jnp.dot(g, wo_ref[:, sl], preferred_element_type=F32)
            z = y if first else o_ref[:, sl] + y
            o_ref[:, sl] = z
            if normalize:
                sq = sq + jnp.sum(z * z, axis=-1, keepdims=True)
        if normalize:
            o_ref[...] = o_ref[...] * lax.rsqrt(sq * (1.0 / D_MODEL) + EPS) * gfin_ref[...]

    last = pl.num_programs(2) - 1

    @pl.when(f == 0)
    def _():
        ffn_tile_sum(True)

    @pl.when((f > 0) & (f < last))
    def _():
        ffn_tile_sum(False)
        r0 = pl.multiple_of((jnp.minimum(f, FFN_X1_CHUNKS) - 1) * chunk_rows, chunk_rows)
        rows = pl.ds(r0, chunk_rows)
        o_ref[rows, :] += jnp.where(f <= FFN_X1_CHUNKS, x1_ref[...], 0.0)

    @pl.when(f == last)
    def _():
        ffn_tile_sum(False, normalize=final_norm)


def _ffn_call(h2, x1, mod, g_final, w_ffn_in, w_ffn_out, tm, tf, final_norm):
    B, S, _ = x1.shape
    nf = D_FF // tf
    row = lambda: pl.BlockSpec((None, tm, D_MODEL), lambda b, i, f: (b, i, 0))
    assert nf - 1 > FFN_X1_CHUNKS and tm % (8 * FFN_X1_CHUNKS) == 0
    x1_spec = pl.BlockSpec(
        (None, tm // FFN_X1_CHUNKS, D_MODEL),
        lambda b, i, f: (b, i * FFN_X1_CHUNKS + jnp.clip(f - 1, 0, FFN_X1_CHUNKS - 1), 0))
    return pl.pallas_call(
        functools.partial(_ffn_kernel, final_norm=final_norm),
        grid=(B, S // tm, nf),
        in_specs=[row(), x1_spec,
                  pl.BlockSpec((None, None, 1, D_MODEL), lambda b, i, f: (b, 5, 0, 0)),
                  pl.BlockSpec((1, D_MODEL), lambda b, i, f: (0, 0)),
                  pl.BlockSpec((D_MODEL, tf), lambda b, i, f: (0, f)),
                  pl.BlockSpec((D_MODEL, tf), lambda b, i, f: (0, f + nf)),
                  pl.BlockSpec((tf, D_MODEL), lambda b, i, f: (f, 0))],
        out_specs=row(),
        out_shape=jax.ShapeDtypeStruct((B, S, D_MODEL), F32),
        compiler_params=_params(("arbitrary", "arbitrary", "arbitrary")),
        name="swiglu_ffn",
    )(h2, x1, mod, g_final, w_ffn_in, w_ffn_in, w_ffn_out)


def _rope_tables(seq):
    inv = 1.0 / (ROPE_THETA ** (jnp.arange(0, HEAD_DIM, 2, dtype=F32) / HEAD_DIM))
    ang = jnp.arange(seq, dtype=F32)[:, None] * inv[None, :]
    cos, sin = jnp.cos(ang), jnp.sin(ang)
    reps = LANES // HEAD_DIM
    cos_t = jnp.tile(jnp.concatenate([cos, cos], axis=-1), (1, reps))
    sin_t = jnp.tile(jnp.concatenate([-sin, sin], axis=-1), (1, reps))
    return cos_t, sin_t


def _encoder(x, mod, layers, g_final):
    B, S, _ = x.shape
    cos_t, sin_t = _rope_tables(S)
    for l, lw in enumerate(layers):
        m = mod[l]
        ugg, qt, kv = _inproj_call(x, m, lw["g_mix"], cos_t, sin_t, lw["w_in"], tm=1024)
        attn = _attn_call(qt, kv, lw["sink"], tq=2048)
        x1, h2 = _mixer_call(x, ugg, attn, m, lw["conv_w"], lw["conv_b"], lw["ln_g"], lw["ln_b"],
                             lw["g_ffn"], lw["wao"], lw["wco"], lw["wout"], tm=256)
        x = _ffn_call(h2, x1, m, g_final, lw["w_ffn_in"], lw["w_ffn_out"], tm=1024, tf=512,
                      final_norm=(l == len(layers) - 1))
    return x


def kernel(x_prompt, x_sample, c_prompt, c_sample, w_ada, b_ada, g_mix, w_in, attn_sink, w_attn_o,
           conv_w, conv_b, conv_ln_g, conv_ln_b, w_conv_o, w_out, g_ffn, w_ffn_in, w_ffn_out, g_final):
    depth = w_in.shape[0]
    nb_p, nb_s = c_prompt.shape[0], c_sample.shape[0]
    rows = -(-(nb_p + nb_s) // 8) * 8
    c_all = jnp.concatenate([c_prompt, c_sample], axis=0)
    c_pad = jnp.pad(c_all, ((0, rows - nb_p - nb_s), (0, 0)))

    layers, mods_p, mods_s = [], [], []
    for l in range(depth):
        mod = _mod_call(c_pad, w_ada[l], b_ada[l]).reshape(rows, N_MOD, 1, D_MODEL)
        mods_p.append(mod[:nb_p])
        mods_s.append(mod[nb_p:nb_p + nb_s])
        layers.append(dict(
            g_mix=g_mix[l].reshape(1, D_MODEL), w_in=w_in[l].astype(BF16), sink=attn_sink[l],
            wao=w_attn_o[l].astype(BF16),
            conv_w=jnp.pad(conv_w[l], ((0, 1), (0, 0))), conv_b=conv_b[l].reshape(1, CONV_DIM),
            ln_g=conv_ln_g[l].reshape(1, CONV_DIM), ln_b=conv_ln_b[l].reshape(1, CONV_DIM),
            wco=w_conv_o[l].astype(BF16), wout=w_out[l].astype(BF16),
            g_ffn=g_ffn[l].reshape(1, D_MODEL), w_ffn_in=w_ffn_in[l].astype(BF16),
            w_ffn_out=w_ffn_out[l].astype(BF16)))
    gfin = g_final.reshape(1, D_MODEL)
    y_prompt = _encoder(x_prompt, mods_p, layers, gfin)
    y_sample = _encoder(x_sample, mods_s, layers, gfin)
    return (y_prompt, y_sample)
```
